```python
import jax, jax.numpy as jnp
from jax import lax
import numpy as np

D_MODEL = 1024
BATCH = 4
SEQ = 8192
DEPTH = 2

N_META = 16
BLOCK_Q = 128
HEAD_DIM = 128
FOX_HEADS = 4
SB_HEADS = 4
FOX_WIDTH = FOX_HEADS * HEAD_DIM
SB_WIDTH = SB_HEADS * HEAD_DIM
LRU_WIDTH = 512
LRU_BLOCKS = 8
LRU_BLOCK_DIM = LRU_WIDTH // LRU_BLOCKS
LRU_C = 8.0
CONV_WIDTH = 4
N_BRANCH = 3
D_FF = 2816
N_EXPERTS = 8
TOP_K = 2
D_FF_EXPERT = 3584
N_DENSE = (DEPTH + 1) // 2
N_MOE = DEPTH // 2
RMS_EPS = 1e-6
NEG = -1e30
SPLIT_SIZES = (3 * FOX_WIDTH, FOX_HEADS, 3 * SB_WIDTH, LRU_WIDTH, LRU_WIDTH, N_BRANCH * D_MODEL)
IN_COLS = 3 * FOX_WIDTH + FOX_HEADS + 3 * SB_WIDTH + 2 * LRU_WIDTH + N_BRANCH * D_MODEL

kernel_name = 'hybrid_fox_stickbreak_rglru_moe'


def rms_norm(x, g):
    xf = x.astype(jnp.float32)
    y = xf * lax.rsqrt(jnp.mean(xf * xf, axis=-1, keepdims=True) + RMS_EPS)
    return (y * g.astype(jnp.float32)).astype(x.dtype)


def _to_blocks(t, nblk):
    b, h, _ = t.shape[:3]
    t = t.reshape((b, h, nblk, BLOCK_Q) + t.shape[3:])
    return jnp.moveaxis(t, 2, 0)


def _from_blocks(o):
    nblk, b, h, bq, dh = o.shape
    return o.transpose(1, 0, 3, 2, 4).reshape(b, nblk * bq, h * dh)


def _front_pad(t, pad):
    return jnp.pad(t, ((0, 0), (pad, 0)) + ((0, 0),) * (t.ndim - 2))


def forgetting_attention(q, k, v, log_f):
    b, l, h, dh = q.shape
    pad = (-l) % BLOCK_Q
    lp = l + pad
    nblk = lp // BLOCK_Q
    c = jnp.cumsum(log_f.astype(jnp.float32), axis=1)
    q, k, v = [_front_pad(t, pad).transpose(0, 2, 1, 3) for t in (q, k, v)]
    c = _front_pad(c, pad).transpose(0, 2, 1)
    kpos = jnp.arange(lp)
    scale = dh ** -0.5

    def block(args):
        qi, ci, i = args
        qpos = i * BLOCK_Q + jnp.arange(BLOCK_Q)
        s = (jnp.einsum('bhqd,bhkd->bhqk', qi, k).astype(jnp.float32) * scale
             + ci[..., None] - c[:, :, None, :])
        mask = (kpos[None, :] <= qpos[:, None]) & (kpos[None, :] >= pad)
        p = jax.nn.softmax(jnp.where(mask, s, NEG), axis=-1)
        return jnp.einsum('bhqk,bhkd->bhqd', p.astype(v.dtype), v)

    o = lax.map(block, (_to_blocks(q, nblk), _to_blocks(c, nblk), jnp.arange(nblk)))
    return _from_blocks(o)[:, pad:]


def stick_breaking_attention(q, k, v):
    b, l, h, dh = q.shape
    pad = (-l) % BLOCK_Q
    lp = l + pad
    nblk = lp // BLOCK_Q
    q, k, v = [_front_pad(t, pad).transpose(0, 2, 1, 3) for t in (q, k, v)]
    kpos = jnp.arange(lp)
    scale = dh ** -0.5

    def block(args):
        qi, i = args
        qpos = i * BLOCK_Q + jnp.arange(BLOCK_Q)
        z = jnp.einsum('bhqd,bhkd->bhqk', qi, k).astype(jnp.float32) * scale
        mask = (kpos[None, :] < qpos[:, None]) & (kpos[None, :] >= pad)
        m = jnp.where(mask, jax.nn.log_sigmoid(-z), 0.0)
        later = lax.cumsum(m, axis=3, reverse=True) - m
        a = jnp.where(mask, jnp.exp(jax.nn.log_sigmoid(z) + later), 0.0)
        return jnp.einsum('bhqk,bhkd->bhqd', a.astype(v.dtype), v)

    o = lax.map(block, (_to_blocks(q, nblk), jnp.arange(nblk)))
    return _from_blocks(o)[:, pad:]


def recurrent_branch(x_in, x_gate, conv_w, conv_b, wa, ba, wx, bx, lam):
    b, l, w = x_in.shape
    u = lax.conv_general_dilated(x_in, conv_w[:, None, :].astype(x_in.dtype), window_strides=(1,),
                                 padding=[(CONV_WIDTH - 1, 0)],
                                 dimension_numbers=('NWC', 'WIO', 'NWC'),
                                 feature_group_count=w) + conv_b
    ub = u.reshape(b, l, LRU_BLOCKS, LRU_BLOCK_DIM)
    r = jax.nn.sigmoid(jnp.einsum('blni,nij->blnj', ub, wa).reshape(b, l, w) + ba)
    gi = jax.nn.sigmoid(jnp.einsum('blni,nij->blnj', ub, wx).reshape(b, l, w) + bx)
    log_a = -LRU_C * r.astype(jnp.float32) * jax.nn.softplus(-lam.astype(jnp.float32))
    a = jnp.exp(log_a)
    inp = jnp.sqrt(-jnp.expm1(2.0 * log_a)) * (gi * u).astype(jnp.float32)

    def combine(e1, e2):
        a1, b1 = e1
        a2, b2 = e2
        return a1 * a2, a2 * b1 + b2

    _, hseq = lax.associative_scan(combine, (a, inp), axis=1)
    return hseq.astype(x_in.dtype) * jax.nn.gelu(x_gate)


def mixer(xn, w_in, b_forget, b_gate, conv_w, conv_b, wa, ba, wx, bx, lam,
          w_fox_o, w_sb_o, w_lru_o, w_out):
    b, l, d = xn.shape
    proj = xn @ w_in
    idx = [int(v) for v in np.cumsum(SPLIT_SIZES)[:-1]]
    fox_qkv, fox_f, sb_qkv, lru_x, lru_g, gates = jnp.split(proj, idx, axis=-1)
    fox_qkv = fox_qkv.reshape(b, l, 3, FOX_HEADS, HEAD_DIM)
    sb_qkv = sb_qkv.reshape(b, l, 3, SB_HEADS, HEAD_DIM)
    log_f = jax.nn.log_sigmoid((fox_f + b_forget).astype(jnp.float32))
    y_fox = forgetting_attention(fox_qkv[:, :, 0], fox_qkv[:, :, 1], fox_qkv[:, :, 2], log_f) @ w_fox_o
    y_sb = stick_breaking_attention(sb_qkv[:, :, 0], sb_qkv[:, :, 1], sb_qkv[:, :, 2]) @ w_sb_o
    y_lru = recurrent_branch(lru_x, lru_g, conv_w, conv_b, wa, ba, wx, bx, lam) @ w_lru_o
    g = jax.nn.sigmoid(gates + b_gate).reshape(b, l, N_BRANCH, d)
    merged = g[:, :, 0] * y_fox + g[:, :, 1] * y_sb + g[:, :, 2] * y_lru
    return merged @ w_out


def swiglu(x, wg, wu, wd):
    return (jax.nn.silu(x @ wg) * (x @ wu)) @ wd


def moe_swiglu(x, router_w, wg, wu, wd):
    b, l, d = x.shape
    xt = x.reshape(-1, d)
    logits = (xt @ router_w).astype(jnp.float32)
    top_v, top_i = lax.top_k(logits, TOP_K)
    top_w = jax.nn.softmax(top_v, axis=-1)
    comb = jnp.sum(jax.nn.one_hot(top_i, N_EXPERTS, dtype=jnp.float32) * top_w[..., None], axis=1)
    comb = comb.astype(x.dtype)
    out = jnp.zeros_like(xt)
    for e in range(N_EXPERTS):
        out = out + comb[:, e:e + 1] * swiglu(xt, wg[e], wu[e], wd[e])
    return out.reshape(b, l, d)


def setup_inputs(seed: int = 0) -> dict:
    key = jax.random.key(seed)
    ks = iter(jax.random.split(key, 40))
    f32 = jnp.float32

    def nrm(shape, scale):
        return jax.random.normal(next(ks), shape, f32) * scale

    u = jax.random.uniform(next(ks), (DEPTH, LRU_WIDTH), f32, 0.9, 0.999)
    a_base = u ** (1.0 / LRU_C)
    lru_lambda = jnp.log(a_base) - jnp.log1p(-a_base)
    return {
        'x': nrm((BATCH, SEQ, D_MODEL), 1.0),
        'meta_tokens': nrm((N_META, D_MODEL), 1.0),
        'g_mix': 1.0 + nrm((DEPTH, D_MODEL), 0.02),
        'w_in': nrm((DEPTH, D_MODEL, IN_COLS), D_MODEL ** -0.5),
        'b_forget': 3.0 + nrm((DEPTH, FOX_HEADS), 0.5),
        'b_gate': nrm((DEPTH, N_BRANCH * D_MODEL), 0.02),
        'conv_w': nrm((DEPTH, CONV_WIDTH, LRU_WIDTH), CONV_WIDTH ** -0.5),
        'conv_b': nrm((DEPTH, LRU_WIDTH), 0.02),
        'lru_wa': nrm((DEPTH, LRU_BLOCKS, LRU_BLOCK_DIM, LRU_BLOCK_DIM), LRU_BLOCK_DIM ** -0.5),
        'lru_ba': nrm((DEPTH, LRU_WIDTH), 0.02),
        'lru_wx': nrm((DEPTH, LRU_BLOCKS, LRU_BLOCK_DIM, LRU_BLOCK_DIM), LRU_BLOCK_DIM ** -0.5),
        'lru_bx': nrm((DEPTH, LRU_WIDTH), 0.02),
        'lru_lambda': lru_lambda,
        'w_fox_o': nrm((DEPTH, FOX_WIDTH, D_MODEL), FOX_WIDTH ** -0.5),
        'w_sb_o': nrm((DEPTH, SB_WIDTH, D_MODEL), SB_WIDTH ** -0.5),
        'w_lru_o': nrm((DEPTH, LRU_WIDTH, D_MODEL), LRU_WIDTH ** -0.5),
        'w_out': nrm((DEPTH, D_MODEL, D_MODEL), D_MODEL ** -0.5),
        'g_ffn': 1.0 + nrm((DEPTH, D_MODEL), 0.02),
        'ffn_w_gate': nrm((N_DENSE, D_MODEL, D_FF), D_MODEL ** -0.5),
        'ffn_w_up': nrm((N_DENSE, D_MODEL, D_FF), D_MODEL ** -0.5),
        'ffn_w_down': nrm((N_DENSE, D_FF, D_MODEL), D_FF ** -0.5),
        'router_w': nrm((N_MOE, D_MODEL, N_EXPERTS), D_MODEL ** -0.5),
        'moe_w_gate': nrm((N_MOE, N_EXPERTS, D_MODEL, D_FF_EXPERT), D_MODEL ** -0.5),
        'moe_w_up': nrm((N_MOE, N_EXPERTS, D_MODEL, D_FF_EXPERT), D_MODEL ** -0.5),
        'moe_w_down': nrm((N_MOE, N_EXPERTS, D_FF_EXPERT, D_MODEL), D_FF_EXPERT ** -0.5),
        'g_final': 1.0 + nrm((D_MODEL,), 0.02),
    }


def reference(x, meta_tokens, g_mix, w_in, b_forget, b_gate, conv_w, conv_b, lru_wa, lru_ba,
              lru_wx, lru_bx, lru_lambda, w_fox_o, w_sb_o, w_lru_o, w_out, g_ffn,
              ffn_w_gate, ffn_w_up, ffn_w_down, router_w, moe_w_gate, moe_w_up, moe_w_down,
              g_final):
    b = x.shape[0]
    meta = jnp.broadcast_to(meta_tokens[None].astype(x.dtype), (b, N_META, x.shape[-1]))
    h = jnp.concatenate([meta, x], axis=1)
    for layer in range(DEPTH):
        h = h + mixer(rms_norm(h, g_mix[layer]), w_in[layer], b_forget[layer], b_gate[layer],
                      conv_w[layer], conv_b[layer], lru_wa[layer], lru_ba[layer],
                      lru_wx[layer], lru_bx[layer], lru_lambda[layer],
                      w_fox_o[layer], w_sb_o[layer], w_lru_o[layer], w_out[layer])
        hn = rms_norm(h, g_ffn[layer])
        j = layer // 2
        if layer % 2 == 0:
            h = h + swiglu(hn, ffn_w_gate[j], ffn_w_up[j], ffn_w_down[j])
        else:
            h = h + moe_swiglu(hn, router_w[j], moe_w_gate[j], moe_w_up[j], moe_w_down[j])
    return rms_norm(h, g_final)[:, N_META:]
```

```python
import functools

import jax
import jax.numpy as jnp
from jax import lax
from jax.experimental import pallas as pl
from jax.experimental.pallas import tpu as pltpu

N_META = 16
HEAD_DIM = 128
FOX_HEADS = 4
SB_HEADS = 4
FOX_WIDTH = FOX_HEADS * HEAD_DIM
SB_WIDTH = SB_HEADS * HEAD_DIM
LRU_WIDTH = 512
LRU_BLOCKS = 8
LRU_C = 8.0
CONV_WIDTH = 4
N_BRANCH = 3
N_EXPERTS = 8
RMS_EPS = 1e-6
NEG = -1e30

LANE = 128
SUBLANE = 8
SEQ_TILE = 768
KV_CHUNK = 256
ROW_TILE = 512
TIME_TILE = 256
MOE_TILE = 512
MOE_FF_TILE = 1792
VMEM_LIMIT = 56 * 1024 * 1024

F32 = jnp.float32
BF16 = jnp.bfloat16


def _params(*sem):
    return pltpu.CompilerParams(dimension_semantics=sem, vmem_limit_bytes=VMEM_LIMIT)


def _resident(shape):
    return pl.BlockSpec(shape, lambda *_: (0,) * len(shape), pipeline_mode=pl.Buffered(1))


def _rms(x, g):
    return x * lax.rsqrt(jnp.mean(x * x, axis=-1, keepdims=True) + RMS_EPS) * g


def _softplus(x):
    return jnp.maximum(x, 0.0) + jnp.log1p(jnp.exp(-jnp.abs(x)))


def _inproj_kernel(h_ref, g_ref, wqkv_ref, wlru_ref, wf_ref, qkv_ref, lru_ref, f_ref):
    xn = _rms(h_ref[...], g_ref[...]).astype(BF16)
    scale = HEAD_DIM ** -0.5
    for c in range(6):
        cols = slice(c * FOX_WIDTH, (c + 1) * FOX_WIDTH)
        r = jnp.dot(xn, wqkv_ref[:, cols], preferred_element_type=F32)
        if c % 3 == 0:
            r = r * scale
        qkv_ref[:, cols] = r.astype(BF16)
    lru_ref[...] = jnp.dot(xn, wlru_ref[...], preferred_element_type=F32)
    f_ref[...] = jnp.dot(xn, wf_ref[...], preferred_element_type=F32)


def _inproj(h, g, wqkv, wlru, wf):
    t, d = h.shape
    tm = ROW_TILE
    return pl.pallas_call(
        _inproj_kernel,
        grid=(t // tm,),
        in_specs=[pl.BlockSpec((tm, d), lambda i: (i, 0)), _resident((1, d)), _resident(wqkv.shape),
                  _resident(wlru.shape), _resident(wf.shape)],
        out_specs=[pl.BlockSpec((tm, wqkv.shape[1]), lambda i: (i, 0)),
                   pl.BlockSpec((tm, wlru.shape[1]), lambda i: (i, 0)),
                   pl.BlockSpec((tm, LANE), lambda i: (i, 0))],
        out_shape=[jax.ShapeDtypeStruct((t, wqkv.shape[1]), BF16),
                   jax.ShapeDtypeStruct((t, wlru.shape[1]), F32),
                   jax.ShapeDtypeStruct((t, LANE), F32)],
        compiler_params=_params("parallel"),
        name="inproj",
    )(h, g, wqkv, wlru, wf)


def _shift_rows(a, d, fill):
    rows = lax.broadcasted_iota(jnp.int32, a.shape, 0)
    return jnp.where(rows >= d, pltpu.roll(a, d, 0), fill)


def _lru_kernel(lru_ref, f_ref, cw_ref, cb_ref, wa_ref, ba_ref, wx_ref, bx_ref, lam_ref, bf_ref,
                y_ref, c_ref, xx_ref, hprev_ref, cprev_ref):
    tt = lru_ref.shape[0]
    w = LRU_WIDTH

    @pl.when(pl.program_id(1) == 0)
    def _():
        xx_ref[0:SUBLANE, :] = jnp.zeros((SUBLANE, w), F32)
        hprev_ref[...] = jnp.zeros_like(hprev_ref)
        cprev_ref[...] = jnp.zeros_like(cprev_ref)

    x = lru_ref[:, :w]
    xx_ref[SUBLANE:SUBLANE + tt, :] = x
    u = cb_ref[...] + cw_ref[3:4, :] * x
    for k in range(CONV_WIDTH - 1):
        u = u + cw_ref[k:k + 1, :] * xx_ref[SUBLANE - 3 + k:SUBLANE - 3 + k + tt, :]
    xx_ref[0:SUBLANE, :] = x[tt - SUBLANE:, :]

    ub = u.astype(BF16)
    r = jax.nn.sigmoid(jnp.dot(ub, wa_ref[...], preferred_element_type=F32) + ba_ref[...])
    gi = jax.nn.sigmoid(jnp.dot(ub, wx_ref[...], preferred_element_type=F32) + bx_ref[...])
    log_a = (-LRU_C) * r * _softplus(-lam_ref[...])
    a = jnp.exp(log_a)
    b = jnp.sqrt(1.0 - a * a) * (gi * u)

    d = 1
    while d < tt:
        b = a * _shift_rows(b, d, 0.0) + b
        a = a * _shift_rows(a, d, 1.0)
        d *= 2
    h = b + a * hprev_ref[...]
    hprev_ref[...] = h[tt - 1:tt, :]
    y_ref[...] = (h * jax.nn.gelu(lru_ref[:, w:])).astype(y_ref.dtype)

    c = -_softplus(-(f_ref[...] + bf_ref[...]))
    d = 1
    while d < tt:
        c = c + _shift_rows(c, d, 0.0)
        d *= 2
    c = c + cprev_ref[...]
    cprev_ref[...] = c[tt - 1:tt, :]
    c_ref[...] = c


def _lru(lru, f, cw, cb, wa, ba, wx, bx, lam, bf, batch):
    t = lru.shape[0]
    lp = t // batch
    tt = TIME_TILE
    nt = lp // tt
    w = LRU_WIDTH
    row = lambda b, i: (b * nt + i, 0)
    return pl.pallas_call(
        _lru_kernel,
        grid=(batch, nt),
        in_specs=[pl.BlockSpec((tt, 2 * w), row), pl.BlockSpec((tt, LANE), row),
                  _resident(cw.shape), _resident((1, w)), _resident((w, w)), _resident((1, w)),
                  _resident((w, w)), _resident((1, w)), _resident((1, w)), _resident((1, LANE))],
        out_specs=[pl.BlockSpec((tt, w), row), pl.BlockSpec((tt, LANE), row)],
        out_shape=[jax.ShapeDtypeStruct((t, w), BF16), jax.ShapeDtypeStruct((t, LANE), F32)],
        scratch_shapes=[pltpu.VMEM((tt + SUBLANE, w), F32), pltpu.VMEM((1, w), F32), pltpu.VMEM((1, LANE), F32)],
        compiler_params=_params("parallel", "arbitrary"),
        name="lru",
    )(lru, f, cw, cb, wa, ba, wx, bx, lam, bf)


def _fox_kernel(q_ref, k_ref, v_ref, cq_ref, ck_ref, o_ref, m_ref, l_ref, acc_ref):
    i = pl.program_id(2)
    tq = q_ref.shape[0]
    tk = KV_CHUNK
    q = q_ref[...]
    cq = cq_ref[...]
    qpos = i * tq + lax.broadcasted_iota(jnp.int32, (tq, 1), 0)
    m_ref[...] = jnp.full_like(m_ref, NEG)
    l_ref[...] = jnp.zeros_like(l_ref)
    acc_ref[...] = jnp.zeros_like(acc_ref)

    def body(j, carry):
        off = pl.multiple_of(j * tk, tk)
        kc = k_ref[pl.ds(off, tk), :]
        vc = v_ref[pl.ds(off, tk), :]
        s = lax.dot_general(q, kc, (((1,), (1,)), ((), ())), preferred_element_type=F32)
        s = s + (cq - ck_ref[j])
        kpos = j * tk + lax.broadcasted_iota(jnp.int32, (1, tk), 1)
        s = jnp.where(kpos <= qpos, s, NEG)
        m_prev = m_ref[...]
        m_new = jnp.maximum(m_prev, jnp.max(s, axis=-1, keepdims=True))
        alpha = jnp.exp(m_prev - m_new)
        p = jnp.exp(s - m_new)
        l_ref[...] = alpha * l_ref[...] + jnp.sum(p, axis=-1, keepdims=True)
        acc_ref[...] = alpha * acc_ref[...] + jnp.dot(p.astype(BF16), vc, preferred_element_type=F32)
        m_ref[...] = m_new
        return carry

    lax.fori_loop(0, (i + 1) * (tq // tk), body, 0)
    o_ref[...] = (acc_ref[...] / l_ref[...]).astype(o_ref.dtype)


def _fox(qkv, c_col, c_row, batch):
    t = qkv.shape[0]
    lp = t // batch
    tq = SEQ_TILE
    nq = lp // tq
    h = FOX_HEADS
    return pl.pallas_call(
        _fox_kernel,
        grid=(batch, h, nq),
        in_specs=[pl.BlockSpec((tq, HEAD_DIM), lambda b, hh, i: (b * nq + i, hh)),
                  pl.BlockSpec((lp, HEAD_DIM), lambda b, hh, i: (b, h + hh)),
                  pl.BlockSpec((lp, HEAD_DIM), lambda b, hh, i: (b, 2 * h + hh)),
                  pl.BlockSpec((None, None, tq, 1), lambda b, hh, i: (b, hh, i, 0)),
                  pl.BlockSpec((None, None, lp // KV_CHUNK, 1, KV_CHUNK), lambda b, hh, i: (b, hh, 0, 0, 0))],
        out_specs=pl.BlockSpec((tq, HEAD_DIM), lambda b, hh, i: (b * nq + i, hh)),
        out_shape=jax.ShapeDtypeStruct((t, FOX_WIDTH), BF16),
        scratch_shapes=[pltpu.VMEM((tq, 1), F32), pltpu.VMEM((tq, 1), F32), pltpu.VMEM((tq, HEAD_DIM), F32)],
        compiler_params=_params("parallel", "parallel", "arbitrary"),
        name="fox",
    )(qkv, qkv, qkv, c_col, c_row)


def _sb_kernel(q_ref, k_ref, v_ref, o_ref, carry_ref, acc_ref):
    i = pl.program_id(2)
    tq = q_ref.shape[0]
    tk = KV_CHUNK
    q = q_ref[...]
    qpos = i * tq + lax.broadcasted_iota(jnp.int32, (tq, 1), 0)
    rows = lax.broadcasted_iota(jnp.int32, (tk, tk), 0)
    cols = lax.broadcasted_iota(jnp.int32, (tk, tk), 1)
    later_mat = jnp.where(rows > cols, 1.0, 0.0).astype(BF16)
    ones_mat = jnp.ones((tk, LANE), BF16)
    carry_ref[...] = jnp.zeros_like(carry_ref)
    acc_ref[...] = jnp.zeros_like(acc_ref)
    n = (i + 1) * (tq // tk)

    def body(it, carry):
        j = n - 1 - it
        off = pl.multiple_of(j * tk, tk)
        kc = k_ref[pl.ds(off, tk), :]
        vc = v_ref[pl.ds(off, tk), :]
        z = lax.dot_general(q, kc, (((1,), (1,)), ((), ())), preferred_element_type=F32)
        kpos = j * tk + lax.broadcasted_iota(jnp.int32, (1, tk), 1)
        mask = kpos < qpos
        sp = _softplus(z)
        m = jnp.where(mask, -sp, 0.0)
        m_hi = m.astype(BF16)
        m_lo = (m - m_hi.astype(F32)).astype(BF16)
        run = carry_ref[...]
        later = (jnp.dot(m_hi, later_mat, preferred_element_type=F32)
                 + jnp.dot(m_lo, later_mat, preferred_element_type=F32)
                 + jnp.concatenate([run] * (tk // LANE), axis=1))
        a = jnp.where(mask, jnp.exp((z - sp) + later), 0.0)
        acc_ref[...] += jnp.dot(a.astype(BF16), vc, preferred_element_type=F32)
        carry_ref[...] = run + (jnp.dot(m_hi, ones_mat, preferred_element_type=F32)
                                + jnp.dot(m_lo, ones_mat, preferred_element_type=F32))
        return carry

    lax.fori_loop(0, n, body, 0)
    o_ref[...] = acc_ref[...].astype(o_ref.dtype)


def _sb(qkv, batch):
    t = qkv.shape[0]
    lp = t // batch
    tq = SEQ_TILE
    nq = lp // tq
    h = SB_HEADS
    base = 3 * FOX_HEADS
    return pl.pallas_call(
        _sb_kernel,
        grid=(batch, h, nq),
        in_specs=[pl.BlockSpec((tq, HEAD_DIM), lambda b, hh, i: (b * nq + i, base + hh)),
                  pl.BlockSpec((lp, HEAD_DIM), lambda b, hh, i: (b, base + h + hh)),
                  pl.BlockSpec((lp, HEAD_DIM), lambda b, hh, i: (b, base + 2 * h + hh))],
        out_specs=pl.BlockSpec((tq, HEAD_DIM), lambda b, hh, i: (b * nq + i, hh)),
        out_shape=jax.ShapeDtypeStruct((t, SB_WIDTH), BF16),
        scratch_shapes=[pltpu.VMEM((tq, LANE), F32), pltpu.VMEM((tq, HEAD_DIM), F32)],
        compiler_params=_params("parallel", "parallel", "arbitrary"),
        name="stickbreak",
    )(qkv, qkv, qkv)


def _merge_kernel(h_ref, g_ref, wgate_ref, bgate_ref, yf_ref, ys_ref, yl_ref, wfo_ref, wso_ref, wlo_ref, wout_ref,
                  o_ref):
    h = h_ref[...]
    d = h.shape[1]
    xn = _rms(h, g_ref[...]).astype(BF16)
    merged = None
    for br, (y_ref, w_ref) in enumerate(((yf_ref, wfo_ref), (ys_ref, wso_ref), (yl_ref, wlo_ref))):
        cols = slice(br * d, (br + 1) * d)
        gate = jax.nn.sigmoid(jnp.dot(xn, wgate_ref[:, cols], preferred_element_type=F32) + bgate_ref[:, cols])
        term = gate * jnp.dot(y_ref[...], w_ref[...], preferred_element_type=F32)
        merged = term if merged is None else merged + term
    o_ref[...] = h + jnp.dot(merged.astype(BF16), wout_ref[...], preferred_element_type=F32)


def _merge(h, g, wgate, bgate, yf, ys, yl, wfo, wso, wlo, wout):
    t, d = h.shape
    tm = ROW_TILE
    row = lambda i: (i, 0)
    return pl.pallas_call(
        _merge_kernel,
        grid=(t // tm,),
        in_specs=[pl.BlockSpec((tm, d), row), _resident((1, d)), _resident(wgate.shape), _resident(bgate.shape),
                  pl.BlockSpec((tm, yf.shape[1]), row), pl.BlockSpec((tm, ys.shape[1]), row),
                  pl.BlockSpec((tm, yl.shape[1]), row),
                  _resident(wfo.shape), _resident(wso.shape), _resident(wlo.shape), _resident(wout.shape)],
        out_specs=pl.BlockSpec((tm, d), row),
        out_shape=jax.ShapeDtypeStruct((t, d), F32),
        compiler_params=_params("parallel"),
        name="merge",
    )(h, g, wgate, bgate, yf, ys, yl, wfo, wso, wlo, wout)


def _ffn_chunk(ff):
    best = LANE
    for c in range(LANE, 1536 + 1, LANE):
        if ff % c == 0:
            best = c
    return best


def _ffn_kernel(h_ref, g_ref, wg_ref, wu_ref, wd_ref, gfin_ref, o_ref, *, final):
    h = h_ref[...]
    hn = _rms(h, g_ref[...]).astype(BF16)
    ff = wg_ref.shape[1]
    fc = _ffn_chunk(ff)
    out = h
    for c in range(ff // fc):
        cols = slice(c * fc, (c + 1) * fc)
        act = jax.nn.silu(jnp.dot(hn, wg_ref[:, cols], preferred_element_type=F32))
        act = act * jnp.dot(hn, wu_ref[:, cols], preferred_element_type=F32)
        out = out + jnp.dot(act.astype(BF16), wd_ref[cols, :], preferred_element_type=F32)
    if final:
        out = _rms(out, gfin_ref[...])
    o_ref[...] = out


def _ffn(h, g, wg, wu, wd, gfin, final):
    t, d = h.shape
    tm = ROW_TILE
    row = lambda i: (i, 0)
    return pl.pallas_call(
        functools.partial(_ffn_kernel, final=final),
        grid=(t // tm,),
        in_specs=[pl.BlockSpec((tm, d), row), _resident((1, d)), _resident(wg.shape), _resident(wu.shape),
                  _resident(wd.shape), _resident((1, d))],
        out_specs=pl.BlockSpec((tm, d), row),
        out_shape=jax.ShapeDtypeStruct((t, d), F32),
        compiler_params=_params("parallel"),
        name="ffn",
    )(h, g, wg, wu, wd, gfin)


def _router_kernel(h_ref, g_ref, wr_ref, hn_ref, idx_ref, wts_ref, cnt_ref, run_ref):
    tm = h_ref.shape[0]

    @pl.when(pl.program_id(0) == 0)
    def _():
        run_ref[...] = jnp.zeros_like(run_ref)

    hn = _rms(h_ref[...], g_ref[...])
    hn_ref[...] = hn
    logits = jnp.dot(hn, wr_ref[...], preferred_element_type=F32, precision=lax.Precision.HIGHEST)
    lane = lax.broadcasted_iota(jnp.int32, (tm, LANE), 1)
    lg = jnp.where(lane < N_EXPERTS, logits, -jnp.inf)
    v1 = jnp.max(lg, axis=-1, keepdims=True)
    i1 = jnp.min(jnp.where(lg == v1, lane, LANE), axis=-1, keepdims=True)
    lg2 = jnp.where(lane == i1, -jnp.inf, lg)
    v2 = jnp.max(lg2, axis=-1, keepdims=True)
    i2 = jnp.min(jnp.where(lg2 == v2, lane, LANE), axis=-1, keepdims=True)
    e = jnp.exp(v2 - v1)
    w1 = 1.0 / (1.0 + e)
    w2 = e * w1
    sel1 = lane == i1
    sel2 = lane == i2
    onehot = jnp.where(sel1, 1.0, 0.0) + jnp.where(sel2, 1.0, 0.0)
    rr = lax.broadcasted_iota(jnp.int32, (tm, tm), 0)
    cc = lax.broadcasted_iota(jnp.int32, (tm, tm), 1)
    before = jnp.where(cc < rr, 1.0, 0.0).astype(BF16)
    excl = jnp.dot(before, onehot.astype(BF16), preferred_element_type=F32) + run_ref[...]
    r1 = jnp.sum(jnp.where(sel1, excl, 0.0), axis=-1, keepdims=True)
    r2 = jnp.sum(jnp.where(sel2, excl, 0.0), axis=-1, keepdims=True)
    run = run_ref[...] + jnp.sum(onehot, axis=0, keepdims=True)
    run_ref[...] = run
    idx = jnp.where(lane == 0, i1, jnp.where(lane == 1, i2, 0))
    rank = jnp.where(lane == 2, r1, jnp.where(lane == 3, r2, 0.0)).astype(jnp.int32)
    idx_ref[...] = idx + rank
    wts_ref[...] = jnp.where(lane == 0, w1, jnp.where(lane == 1, w2, 0.0))
    cnt_ref[...] = jnp.broadcast_to(run, cnt_ref.shape)


def _router(h, g, wr):
    t, d = h.shape
    tm = ROW_TILE
    row = lambda i: (i, 0)
    return pl.pallas_call(
        _router_kernel,
        grid=(t // tm,),
        in_specs=[pl.BlockSpec((tm, d), row), _resident((1, d)), _resident(wr.shape)],
        out_specs=[pl.BlockSpec((tm, d), row), pl.BlockSpec((tm, LANE), row), pl.BlockSpec((tm, LANE), row),
                   pl.BlockSpec((SUBLANE, LANE), lambda i: (0, 0))],
        out_shape=[jax.ShapeDtypeStruct((t, d), F32), jax.ShapeDtypeStruct((t, LANE), jnp.int32),
                   jax.ShapeDtypeStruct((t, LANE), F32), jax.ShapeDtypeStruct((SUBLANE, LANE), F32)],
        scratch_shapes=[pltpu.VMEM((1, LANE), F32)],
        compiler_params=_params("arbitrary"),
        name="router",
    )(h, g, wr)


def _row_copy(src, s, dst, d, sem):
    return pltpu.make_async_copy(src.at[pl.ds(s, 1)], dst.at[pl.ds(d, 1)], sem)


def _dispatch_kernel(pos_ref, hn_hbm, xs_in_hbm, xs_hbm, sem):
    del xs_in_hbm
    i = pl.program_id(0)
    tm = pos_ref.shape[1] // 2

    def issue(r, carry):
        src = i * tm + r
        _row_copy(hn_hbm, src, xs_hbm, pos_ref[i, r], sem).start()
        _row_copy(hn_hbm, src, xs_hbm, pos_ref[i, tm + r], sem).start()
        return carry

    lax.fori_loop(0, tm, issue, 0)
    pltpu.make_async_copy(hn_hbm.at[pl.ds(0, 2 * tm)], xs_hbm.at[pl.ds(0, 2 * tm)], sem).wait()


def _dispatch(pos, hn, xs_zero):
    nt = pos.shape[0]
    return pl.pallas_call(
        _dispatch_kernel,
        grid_spec=pltpu.PrefetchScalarGridSpec(
            num_scalar_prefetch=1, grid=(nt,),
            in_specs=[pl.BlockSpec(memory_space=pl.ANY), pl.BlockSpec(memory_space=pl.ANY)],
            out_specs=pl.BlockSpec(memory_space=pl.ANY),
            scratch_shapes=[pltpu.SemaphoreType.DMA(())]),
        out_shape=jax.ShapeDtypeStruct(xs_zero.shape, xs_zero.dtype),
        input_output_aliases={2: 0},
        compiler_params=pltpu.CompilerParams(dimension_semantics=("arbitrary",), has_side_effects=True),
        name="dispatch",
    )(pos, hn, xs_zero)


def _experts_kernel(te_ref, na_ref, xs_ref, wg_ref, wu_ref, wd_ref, ys_ref, xb_ref, acc_ref):
    del te_ref
    i = pl.program_id(0)
    f = pl.program_id(1)

    @pl.when(f == 0)
    def _():
        xb_ref[...] = xs_ref[...].astype(BF16)
        acc_ref[...] = jnp.zeros_like(acc_ref)

    @pl.when(i < na_ref[0])
    def _():
        xb = xb_ref[...]
        act = jax.nn.silu(jnp.dot(xb, wg_ref[...], preferred_element_type=F32))
        act = act * jnp.dot(xb, wu_ref[...], preferred_element_type=F32)
        acc_ref[...] += jnp.dot(act.astype(BF16), wd_ref[...], preferred_element_type=F32)

    @pl.when(f == pl.num_programs(1) - 1)
    def _():
        ys_ref[...] = acc_ref[...]


def _experts(tile_expert, n_active, xs, wg, wu, wd):
    nr, d = xs.shape
    tm = MOE_TILE
    ff = wg.shape[2]
    tf = MOE_FF_TILE if ff % MOE_FF_TILE == 0 else ff
    return pl.pallas_call(
        _experts_kernel,
        grid_spec=pltpu.PrefetchScalarGridSpec(
            num_scalar_prefetch=2, grid=(nr // tm, ff // tf),
            in_specs=[pl.BlockSpec((tm, d), lambda i, f, te, na: (i, 0)),
                      pl.BlockSpec((None, d, tf), lambda i, f, te, na: (te[i], 0, f)),
                      pl.BlockSpec((None, d, tf), lambda i, f, te, na: (te[i], 0, f)),
                      pl.BlockSpec((None, tf, d), lambda i, f, te, na: (te[i], f, 0))],
            out_specs=pl.BlockSpec((tm, d), lambda i, f, te, na: (i, 0)),
            scratch_shapes=[pltpu.VMEM((tm, d), BF16), pltpu.VMEM((tm, d), F32)]),
        out_shape=jax.ShapeDtypeStruct((nr, d), F32),
        compiler_params=_params("arbitrary", "arbitrary"),
        name="experts",
    )(tile_expert, n_active, xs, wg, wu, wd)


def _combine_kernel(pos_ref, h_ref, wts_ref, gfin_ref, ys_hbm, o_ref, buf_ref, sem, *, final):
    i = pl.program_id(0)
    tm = h_ref.shape[0]

    def issue(r, carry):
        _row_copy(ys_hbm, pos_ref[i, r], buf_ref, r, sem).start()
        _row_copy(ys_hbm, pos_ref[i, tm + r], buf_ref, tm + r, sem).start()
        return carry

    lax.fori_loop(0, tm, issue, 0)
    pltpu.make_async_copy(ys_hbm.at[pl.ds(0, 2 * tm)], buf_ref, sem).wait()
    wts = wts_ref[...]
    out = h_ref[...] + wts[:, 0:1] * buf_ref[0:tm, :] + wts[:, 1:2] * buf_ref[tm:2 * tm, :]
    if final:
        out = _rms(out, gfin_ref[...])
    o_ref[...] = out


def _combine(pos, h, wts, gfin, ys, final):
    t, d = h.shape
    nt = pos.shape[0]
    tm = t // nt
    row = lambda i, p: (i, 0)
    return pl.pallas_call(
        functools.partial(_combine_kernel, final=final),
        grid_spec=pltpu.PrefetchScalarGridSpec(
            num_scalar_prefetch=1, grid=(nt,),
            in_specs=[pl.BlockSpec((tm, d), row), pl.BlockSpec((tm, LANE), row),
                      pl.BlockSpec((1, d), lambda i, p: (0, 0)), pl.BlockSpec(memory_space=pl.ANY)],
            out_specs=pl.BlockSpec((tm, d), row),
            scratch_shapes=[pltpu.VMEM((2 * tm, d), F32), pltpu.SemaphoreType.DMA(())]),
        out_shape=jax.ShapeDtypeStruct((t, d), F32),
        compiler_params=_params("arbitrary"),
        name="combine",
    )(pos, h, wts, gfin, ys)


def _moe(h, g, router_w, wg, wu, wd, gfin, final):
    t, d = h.shape
    tm = MOE_TILE
    wr = jnp.pad(router_w, ((0, 0), (0, LANE - N_EXPERTS)))
    hn, idx, wts, cnt = _router(h, g, wr)
    counts = cnt[0, :N_EXPERTS].astype(jnp.int32)
    padded = ((counts + tm - 1) // tm) * tm
    ends = jnp.cumsum(padded)
    offs = ends - padded
    n_tiles = (2 * t + N_EXPERTS * (tm - 1)) // tm
    pos1 = jnp.take(offs, idx[:, 0]) + idx[:, 2]
    pos2 = jnp.take(offs, idx[:, 1]) + idx[:, 3]
    pos = jnp.concatenate([pos1.reshape(-1, ROW_TILE), pos2.reshape(-1, ROW_TILE)], axis=1)
    tile_start = jnp.arange(n_tiles, dtype=jnp.int32) * tm
    tile_expert = jnp.minimum(jnp.sum(tile_start[:, None] >= ends[None, :], axis=1), N_EXPERTS - 1).astype(jnp.int32)
    n_active = (ends[-1:] // tm).astype(jnp.int32)
    xs = _dispatch(pos, hn, jnp.zeros((n_tiles * tm, d), F32))
    ys = _experts(tile_expert, n_active, xs, wg, wu, wd)
    return _combine(pos, h, wts, gfin, ys, final)


def _block_diag(w):
    n, a, b = w.shape
    eye = jnp.eye(n, dtype=w.dtype)
    return jnp.einsum('nij,nm->nimj', w, eye).reshape(n * a, n * b)


def kernel(x, meta_tokens, g_mix, w_in, b_forget, b_gate, conv_w, conv_b, lru_wa, lru_ba, lru_wx, lru_bx, lru_lambda, w_fox_o, w_sb_o, w_lru_o, w_out, g_ffn, ffn_w_gate, ffn_w_up, ffn_w_down, router_w, moe_w_gate, moe_w_up, moe_w_down, g_final):
    batch, seq, d = x.shape
    depth = g_mix.shape[0]
    l = seq + N_META
    lp = -(-l // SEQ_TILE) * SEQ_TILE
    assert (batch * lp) % ROW_TILE == 0 and lp % TIME_TILE == 0 and SEQ_TILE % KV_CHUNK == 0
    meta = jnp.broadcast_to(meta_tokens[None].astype(x.dtype), (batch, N_META, d))
    h = jnp.concatenate([meta, x, jnp.zeros((batch, lp - l, d), x.dtype)], axis=1).reshape(batch * lp, d)

    f0 = 3 * FOX_WIDTH
    s0 = f0 + FOX_HEADS
    x0 = s0 + 3 * SB_WIDTH
    g0 = x0 + 2 * LRU_WIDTH
    gfin = g_final.reshape(1, d)
    for layer in range(depth):
        wl = w_in[layer]
        wqkv = jnp.concatenate([wl[:, :f0], wl[:, s0:x0]], axis=1).astype(BF16)
        wlru = wl[:, x0:g0].astype(BF16)
        wf = jnp.pad(wl[:, f0:s0], ((0, 0), (0, LANE - FOX_HEADS))).astype(BF16)
        wgate = wl[:, g0:].astype(BF16)
        bf = jnp.pad(b_forget[layer], (0, LANE - FOX_HEADS)).reshape(1, LANE)

        qkv, lru, f = _inproj(h, g_mix[layer].reshape(1, d), wqkv, wlru, wf)
        yl, c = _lru(lru, f, conv_w[layer], conv_b[layer].reshape(1, -1),
                     _block_diag(lru_wa[layer]).astype(BF16), lru_ba[layer].reshape(1, -1),
                     _block_diag(lru_wx[layer]).astype(BF16), lru_bx[layer].reshape(1, -1),
                     lru_lambda[layer].reshape(1, -1), bf, batch)
        ch = c[:, :FOX_HEADS].reshape(batch, lp, FOX_HEADS).transpose(0, 2, 1)
        yf = _fox(qkv, ch.reshape(batch, FOX_HEADS, lp, 1),
                  ch.reshape(batch, FOX_HEADS, lp // KV_CHUNK, 1, KV_CHUNK), batch)
        ys = _sb(qkv, batch)
        h = _merge(h, g_mix[layer].reshape(1, d), wgate, b_gate[layer].reshape(1, -1), yf, ys, yl,
                   w_fox_o[layer].astype(BF16), w_sb_o[layer].astype(BF16), w_lru_o[layer].astype(BF16),
                   w_out[layer].astype(BF16))

        final = layer == depth - 1
        j = layer // 2
        gf = g_ffn[layer].reshape(1, d)
        if layer % 2 == 0:
            h = _ffn(h, gf, ffn_w_gate[j].astype(BF16), ffn_w_up[j].astype(BF16), ffn_w_down[j].astype(BF16),
                     gfin, final)
        else:
            h = _moe(h, gf, router_w[j], moe_w_gate[j].astype(BF16), moe_w_up[j].astype(BF16),
                     moe_w_down[j].astype(BF16), gfin, final)
    return h.reshape(batch, lp, d)[:, N_META:l]
```

```python
import functools

import jax
import jax.numpy as jnp
from jax import lax
from jax.experimental import pallas as pl
from jax.experimental.pallas import tpu as pltpu

N_META = 16
HEAD_DIM = 128
FOX_HEADS = 4
SB_HEADS = 4
FOX_WIDTH = FOX_HEADS * HEAD_DIM
SB_WIDTH = SB_HEADS * HEAD_DIM
LRU_WIDTH = 512
LRU_BLOCKS = 8
LRU_C = 8.0
CONV_WIDTH = 4
N_BRANCH = 3
N_EXPERTS = 8
RMS_EPS = 1e-6
NEG = -1e30
LOG2E = 1.4426950408889634

LANE = 128
SUBLANE = 8
SEQ_TILE = 768
KV_CHUNK = 256
ROW_TILE = 512
TIME_TILE = 256
MOE_TILE = 512
MOE_FF_TILE = 1792
VMEM_LIMIT = 56 * 1024 * 1024

F32 = jnp.float32
BF16 = jnp.bfloat16


def _params(*sem):
    return pltpu.CompilerParams(dimension_semantics=sem, vmem_limit_bytes=VMEM_LIMIT)


def _resident(shape):
    return pl.BlockSpec(shape, lambda *_: (0,) * len(shape), pipeline_mode=pl.Buffered(1))


def _rms(x, g):
    return x * lax.rsqrt(jnp.mean(x * x, axis=-1, keepdims=True) + RMS_EPS) * g


def _softplus(x):
    return jnp.maximum(x, 0.0) + jnp.log1p(jnp.exp(-jnp.abs(x)))


def _softplus_scores(x):
    return jnp.maximum(x, 0.0) + jnp.log(1.0 + jnp.exp(-jnp.abs(x)))


def _inproj_kernel(h_ref, g_ref, wqkv_ref, wlru_ref, wf_ref, qkv_ref, lru_ref, f_ref):
    xn = _rms(h_ref[...], g_ref[...]).astype(BF16)
    scale = HEAD_DIM ** -0.5 * LOG2E
    for c in range(6):
        cols = slice(c * FOX_WIDTH, (c + 1) * FOX_WIDTH)
        r = jnp.dot(xn, wqkv_ref[:, cols], preferred_element_type=F32)
        if c % 3 == 0:
            r = r * scale
        qkv_ref[:, cols] = r.astype(BF16)
    lru_ref[...] = jnp.dot(xn, wlru_ref[...], preferred_element_type=F32)
    f_ref[...] = jnp.dot(xn, wf_ref[...], preferred_element_type=F32)


def _inproj(h, g, wqkv, wlru, wf):
    t, d = h.shape
    tm = ROW_TILE
    return pl.pallas_call(
        _inproj_kernel,
        grid=(t // tm,),
        in_specs=[pl.BlockSpec((tm, d), lambda i: (i, 0)), _resident((1, d)), _resident(wqkv.shape),
                  _resident(wlru.shape), _resident(wf.shape)],
        out_specs=[pl.BlockSpec((tm, wqkv.shape[1]), lambda i: (i, 0)),
                   pl.BlockSpec((tm, wlru.shape[1]), lambda i: (i, 0)),
                   pl.BlockSpec((tm, LANE), lambda i: (i, 0))],
        out_shape=[jax.ShapeDtypeStruct((t, wqkv.shape[1]), BF16),
                   jax.ShapeDtypeStruct((t, wlru.shape[1]), F32),
                   jax.ShapeDtypeStruct((t, LANE), F32)],
        compiler_params=_params("parallel"),
        name="inproj",
    )(h, g, wqkv, wlru, wf)


def _shift_rows(a, d, fill):
    rows = lax.broadcasted_iota(jnp.int32, a.shape, 0)
    return jnp.where(rows >= d, pltpu.roll(a, d, 0), fill)


def _lru_kernel(lru_ref, f_ref, cw_ref, cb_ref, wa_ref, ba_ref, wx_ref, bx_ref, lam_ref, bf_ref,
                y_ref, qa_ref, ka_ref, xx_ref, hprev_ref, cprev_ref):
    tt = lru_ref.shape[0]
    w = LRU_WIDTH

    @pl.when(pl.program_id(1) == 0)
    def _():
        xx_ref[0:SUBLANE, :] = jnp.zeros((SUBLANE, w), F32)
        hprev_ref[...] = jnp.zeros_like(hprev_ref)
        cprev_ref[...] = jnp.zeros_like(cprev_ref)

    x = lru_ref[:, :w]
    xx_ref[SUBLANE:SUBLANE + tt, :] = x
    u = cb_ref[...] + cw_ref[3:4, :] * x
    for k in range(CONV_WIDTH - 1):
        u = u + cw_ref[k:k + 1, :] * xx_ref[SUBLANE - 3 + k:SUBLANE - 3 + k + tt, :]
    xx_ref[0:SUBLANE, :] = x[tt - SUBLANE:, :]

    ub = u.astype(BF16)
    r = jax.nn.sigmoid(jnp.dot(ub, wa_ref[...], preferred_element_type=F32) + ba_ref[...])
    gi = jax.nn.sigmoid(jnp.dot(ub, wx_ref[...], preferred_element_type=F32) + bx_ref[...])
    log_a = (-LRU_C) * r * _softplus(-lam_ref[...])
    a = jnp.exp(log_a)
    b = jnp.sqrt(1.0 - a * a) * (gi * u)

    d = 1
    while d < tt:
        b = a * _shift_rows(b, d, 0.0) + b
        a = a * _shift_rows(a, d, 1.0)
        d *= 2
    h = b + a * hprev_ref[...]
    hprev_ref[...] = h[tt - 1:tt, :]
    y_ref[...] = (h * jax.nn.gelu(lru_ref[:, w:])).astype(y_ref.dtype)

    c = -_softplus(-(f_ref[...] + bf_ref[...]))
    d = 1
    while d < tt:
        c = c + _shift_rows(c, d, 0.0)
        d *= 2
    c = c + cprev_ref[...]
    cprev_ref[...] = c[tt - 1:tt, :]
    lane = lax.broadcasted_iota(jnp.int32, (tt, LANE), 1)
    term = jnp.where(lane >= 3, lane - 3, lane)
    for hd in range(FOX_HEADS):
        ch = jnp.broadcast_to(c[:, hd:hd + 1], (tt, LANE)) * LOG2E
        hi = ch.astype(BF16).astype(F32)
        mid = (ch - hi).astype(BF16).astype(F32)
        lo = ((ch - hi) - mid).astype(BF16).astype(F32)
        c3 = jnp.where(term == 0, hi, jnp.where(term == 1, mid, lo))
        cols = slice(hd * LANE, (hd + 1) * LANE)
        qa_ref[:, cols] = jnp.where(lane < 3, c3, jnp.where(lane < 6, 1.0, 0.0)).astype(BF16)
        ka_ref[:, cols] = jnp.where(lane < 3, 1.0, jnp.where(lane < 6, -c3, 0.0)).astype(BF16)


def _lru(lru, f, cw, cb, wa, ba, wx, bx, lam, bf, batch):
    t = lru.shape[0]
    lp = t // batch
    tt = TIME_TILE
    nt = lp // tt
    w = LRU_WIDTH
    row = lambda b, i: (b * nt + i, 0)
    return pl.pallas_call(
        _lru_kernel,
        grid=(batch, nt),
        in_specs=[pl.BlockSpec((tt, 2 * w), row), pl.BlockSpec((tt, LANE), row),
                  _resident(cw.shape), _resident((1, w)), _resident((w, w)), _resident((1, w)),
                  _resident((w, w)), _resident((1, w)), _resident((1, w)), _resident((1, LANE))],
        out_specs=[pl.BlockSpec((tt, w), row), pl.BlockSpec((tt, FOX_WIDTH), row),
                   pl.BlockSpec((tt, FOX_WIDTH), row)],
        out_shape=[jax.ShapeDtypeStruct((t, w), BF16), jax.ShapeDtypeStruct((t, FOX_WIDTH), BF16),
                   jax.ShapeDtypeStruct((t, FOX_WIDTH), BF16)],
        scratch_shapes=[pltpu.VMEM((tt + SUBLANE, w), F32), pltpu.VMEM((1, w), F32), pltpu.VMEM((1, LANE), F32)],
        compiler_params=_params("parallel", "arbitrary"),
        name="lru",
    )(lru, f, cw, cb, wa, ba, wx, bx, lam, bf)


def _wide(a, width):
    return jnp.concatenate([a] * (width // LANE), axis=1)


def _fox_kernel(q_ref, qa_ref, k_ref, ka_ref, v_ref, o_ref, m_ref, acc_ref):
    i = pl.program_id(1)
    tq = q_ref.shape[0]
    tk = KV_CHUNK
    nd = tq // tk
    hd = HEAD_DIM
    ones = jnp.ones((tk, hd), BF16)
    m_ref[...] = jnp.full_like(m_ref, NEG)
    acc_ref[...] = jnp.zeros_like(acc_ref)

    def step(j, r0):
        off = pl.multiple_of(j * tk, tk)
        rows = slice(r0 or 0, tq)
        for h in range(FOX_HEADS):
            cols = slice(h * hd, (h + 1) * hd)
            q2 = jnp.concatenate([q_ref[rows, cols], qa_ref[rows, cols]], axis=1)
            k2 = jnp.concatenate([k_ref[pl.ds(off, tk), cols], ka_ref[pl.ds(off, tk), cols]], axis=1)
            v2 = jnp.concatenate([v_ref[pl.ds(off, tk), cols], ones], axis=1)
            s = lax.dot_general(q2, k2, (((1,), (1,)), ((), ())), preferred_element_type=F32)
            if r0 is not None:
                qpos = i * tq + r0 + lax.broadcasted_iota(jnp.int32, (tq - r0, 1), 0)
                kpos = j * tk + lax.broadcasted_iota(jnp.int32, (1, tk), 1)
                s = jnp.where(kpos <= qpos, s, NEG)
            m_prev = m_ref[h, rows, :]
            m_new = jnp.maximum(m_prev, jnp.max(s, axis=-1, keepdims=True))
            alpha = jnp.exp2(m_prev - m_new)
            p = jnp.exp2(s - _wide(m_new, tk))
            acc_ref[h, rows, :] = (_wide(alpha, 2 * hd) * acc_ref[h, rows, :]
                                   + jnp.dot(p.astype(BF16), v2, preferred_element_type=F32))
            m_ref[h, rows, :] = m_new

    def body(j, carry):
        step(j, None)
        return carry

    lax.fori_loop(0, i * nd, body, 0)
    for dd in range(nd):
        step(i * nd + dd, dd * tk)
    for h in range(FOX_HEADS):
        o_ref[:, h * hd:(h + 1) * hd] = (acc_ref[h, :, :hd] / acc_ref[h, :, hd:]).astype(o_ref.dtype)


def _seq_resident(lp, width, col):
    return pl.BlockSpec((lp, width), lambda b, i: (b, col), pipeline_mode=pl.Buffered(1))


def _fox(qkv, qaug, kaug, batch):
    t = qkv.shape[0]
    lp = t // batch
    tq = SEQ_TILE
    nq = lp // tq
    w = FOX_WIDTH
    qmap = lambda b, i: (b * nq + i, 0)
    return pl.pallas_call(
        _fox_kernel,
        grid=(batch, nq),
        in_specs=[pl.BlockSpec((tq, w), qmap), pl.BlockSpec((tq, w), qmap),
                  _seq_resident(lp, w, 1), _seq_resident(lp, w, 0), _seq_resident(lp, w, 2)],
        out_specs=pl.BlockSpec((tq, w), qmap),
        out_shape=jax.ShapeDtypeStruct((t, w), BF16),
        scratch_shapes=[pltpu.VMEM((FOX_HEADS, tq, LANE), F32), pltpu.VMEM((FOX_HEADS, tq, 2 * HEAD_DIM), F32)],
        compiler_params=_params("parallel", "arbitrary"),
        name="fox",
    )(qkv, qaug, qkv, kaug, qkv)


def _sb_kernel(q_ref, k_ref, v_ref, o_ref, carry_ref, acc_ref):
    i = pl.program_id(1)
    tq = q_ref.shape[0]
    tk = KV_CHUNK
    nd = tq // tk
    hd = HEAD_DIM
    rr = lax.broadcasted_iota(jnp.int32, (tk, tk), 0)
    cc = lax.broadcasted_iota(jnp.int32, (tk, tk), 1)
    later_mat = jnp.where(rr > cc, 1.0, 0.0).astype(BF16)
    carry_ref[...] = jnp.zeros_like(carry_ref)
    acc_ref[...] = jnp.zeros_like(acc_ref)

    def step(j, r0):
        off = pl.multiple_of(j * tk, tk)
        rows = slice(r0 or 0, tq)
        for h in range(SB_HEADS):
            cols = slice(h * hd, (h + 1) * hd)
            z = lax.dot_general(q_ref[rows, cols], k_ref[pl.ds(off, tk), cols], (((1,), (1,)), ((), ())),
                                preferred_element_type=F32)
            sp = jnp.maximum(z, 0.0) + jnp.log2(1.0 + jnp.exp2(-jnp.abs(z)))
            spm = sp
            if r0 is not None:
                qpos = i * tq + r0 + lax.broadcasted_iota(jnp.int32, (tq - r0, 1), 0)
                mask = (j * tk + lax.broadcasted_iota(jnp.int32, (1, tk), 1)) < qpos
                spm = jnp.where(mask, sp, 0.0)
            cum = (jnp.dot(spm.astype(BF16), later_mat, preferred_element_type=F32)
                   + _wide(carry_ref[h, rows, :], tk))
            a = jnp.exp2((z - sp) - cum)
            if r0 is not None:
                a = jnp.where(mask, a, 0.0)
            acc_ref[h, rows, :] += jnp.dot(a.astype(BF16), v_ref[pl.ds(off, tk), cols], preferred_element_type=F32)
            carry_ref[h, rows, :] = jnp.broadcast_to(cum[:, 0:1] + spm[:, 0:1], (tq - rows.start, LANE))

    for dd in range(nd - 1, -1, -1):
        step(i * nd + dd, dd * tk)

    def body(it, carry):
        step(i * nd - 1 - it, None)
        return carry

    lax.fori_loop(0, i * nd, body, 0)
    for h in range(SB_HEADS):
        o_ref[:, h * hd:(h + 1) * hd] = acc_ref[h].astype(o_ref.dtype)


def _sb(qkv, batch):
    t = qkv.shape[0]
    lp = t // batch
    tq = SEQ_TILE
    nq = lp // tq
    w = SB_WIDTH
    base = 3 * FOX_WIDTH // w
    return pl.pallas_call(
        _sb_kernel,
        grid=(batch, nq),
        in_specs=[pl.BlockSpec((tq, w), lambda b, i: (b * nq + i, base)),
                  _seq_resident(lp, w, base + 1), _seq_resident(lp, w, base + 2)],
        out_specs=pl.BlockSpec((tq, w), lambda b, i: (b * nq + i, 0)),
        out_shape=jax.ShapeDtypeStruct((t, w), BF16),
        scratch_shapes=[pltpu.VMEM((SB_HEADS, tq, LANE), F32), pltpu.VMEM((SB_HEADS, tq, HEAD_DIM), F32)],
        compiler_params=_params("parallel", "arbitrary"),
        name="stickbreak",
    )(qkv, qkv, qkv)


def _merge_kernel(h_ref, g_ref, wgate_ref, bgate_ref, yf_ref, ys_ref, yl_ref, wfo_ref, wso_ref, wlo_ref, wout_ref,
                  o_ref):
    h = h_ref[...]
    d = h.shape[1]
    xn = _rms(h, g_ref[...]).astype(BF16)
    merged = None
    for br, (y_ref, w_ref) in enumerate(((yf_ref, wfo_ref), (ys_ref, wso_ref), (yl_ref, wlo_ref))):
        cols = slice(br * d, (br + 1) * d)
        gate = jax.nn.sigmoid(jnp.dot(xn, wgate_ref[:, cols], preferred_element_type=F32) + bgate_ref[:, cols])
        term = gate * jnp.dot(y_ref[...], w_ref[...], preferred_element_type=F32)
        merged = term if merged is None else merged + term
    o_ref[...] = h + jnp.dot(merged.astype(BF16), wout_ref[...], preferred_element_type=F32)


def _merge(h, g, wgate, bgate, yf, ys, yl, wfo, wso, wlo, wout):
    t, d = h.shape
    tm = ROW_TILE
    row = lambda i: (i, 0)
    return pl.pallas_call(
        _merge_kernel,
        grid=(t // tm,),
        in_specs=[pl.BlockSpec((tm, d), row), _resident((1, d)), _resident(wgate.shape), _resident(bgate.shape),
                  pl.BlockSpec((tm, yf.shape[1]), row), pl.BlockSpec((tm, ys.shape[1]), row),
                  pl.BlockSpec((tm, yl.shape[1]), row),
                  _resident(wfo.shape), _resident(wso.shape), _resident(wlo.shape), _resident(wout.shape)],
        out_specs=pl.BlockSpec((tm, d), row),
        out_shape=jax.ShapeDtypeStruct((t, d), F32),
        compiler_params=_params("parallel"),
        name="merge",
    )(h, g, wgate, bgate, yf, ys, yl, wfo, wso, wlo, wout)


def _ffn_chunk(ff):
    best = LANE
    for c in range(LANE, 1536 + 1, LANE):
        if ff % c == 0:
            best = c
    return best


def _ffn_kernel(h_ref, g_ref, wg_ref, wu_ref, wd_ref, gfin_ref, o_ref, *, final):
    h = h_ref[...]
    hn = _rms(h, g_ref[...]).astype(BF16)
    ff = wg_ref.shape[1]
    fc = _ffn_chunk(ff)
    out = h
    for c in range(ff // fc):
        cols = slice(c * fc, (c + 1) * fc)
        act = jax.nn.silu(jnp.dot(hn, wg_ref[:, cols], preferred_element_type=F32))
        act = act * jnp.dot(hn, wu_ref[:, cols], preferred_element_type=F32)
        out = out + jnp.dot(act.astype(BF16), wd_ref[cols, :], preferred_element_type=F32)
    if final:
        out = _rms(out, gfin_ref[...])
    o_ref[...] = out


def _ffn(h, g, wg, wu, wd, gfin, final):
    t, d = h.shape
    tm = ROW_TILE
    row = lambda i: (i, 0)
    return pl.pallas_call(
        functools.partial(_ffn_kernel, final=final),
        grid=(t // tm,),
        in_specs=[pl.BlockSpec((tm, d), row), _resident((1, d)), _resident(wg.shape), _resident(wu.shape),
                  _resident(wd.shape), _resident((1, d))],
        out_specs=pl.BlockSpec((tm, d), row),
        out_shape=jax.ShapeDtypeStruct((t, d), F32),
        compiler_params=_params("parallel"),
        name="ffn",
    )(h, g, wg, wu, wd, gfin)


def _router_kernel(h_ref, g_ref, wr_ref, hn_ref, idx_ref, wts_ref, cnt_ref, run_ref):
    tm = h_ref.shape[0]

    @pl.when(pl.program_id(0) == 0)
    def _():
        run_ref[...] = jnp.zeros_like(run_ref)

    hn = _rms(h_ref[...], g_ref[...])
    hn_ref[...] = hn
    logits = jnp.dot(hn, wr_ref[...], preferred_element_type=F32, precision=lax.Precision.HIGHEST)
    lane = lax.broadcasted_iota(jnp.int32, (tm, LANE), 1)
    lg = jnp.where(lane < N_EXPERTS, logits, -jnp.inf)
    v1 = jnp.max(lg, axis=-1, keepdims=True)
    i1 = jnp.min(jnp.where(lg == v1, lane, LANE), axis=-1, keepdims=True)
    lg2 = jnp.where(lane == i1, -jnp.inf, lg)
    v2 = jnp.max(lg2, axis=-1, keepdims=True)
    i2 = jnp.min(jnp.where(lg2 == v2, lane, LANE), axis=-1, keepdims=True)
    e = jnp.exp(v2 - v1)
    w1 = 1.0 / (1.0 + e)
    w2 = e * w1
    sel1 = lane == i1
    sel2 = lane == i2
    onehot = jnp.where(sel1, 1.0, 0.0) + jnp.where(sel2, 1.0, 0.0)
    rr = lax.broadcasted_iota(jnp.int32, (tm, tm), 0)
    cc = lax.broadcasted_iota(jnp.int32, (tm, tm), 1)
    before = jnp.where(cc < rr, 1.0, 0.0).astype(BF16)
    excl = jnp.dot(before, onehot.astype(BF16), preferred_element_type=F32) + run_ref[...]
    r1 = jnp.sum(jnp.where(sel1, excl, 0.0), axis=-1, keepdims=True)
    r2 = jnp.sum(jnp.where(sel2, excl, 0.0), axis=-1, keepdims=True)
    run = run_ref[...] + jnp.sum(onehot, axis=0, keepdims=True)
    run_ref[...] = run
    idx = jnp.where(lane == 0, i1, jnp.where(lane == 1, i2, 0))
    rank = jnp.where(lane == 2, r1, jnp.where(lane == 3, r2, 0.0)).astype(jnp.int32)
    idx_ref[...] = idx + rank
    wts_ref[...] = jnp.where(lane == 0, w1, jnp.where(lane == 1, w2, 0.0))
    cnt_ref[...] = jnp.broadcast_to(run, cnt_ref.shape)


def _router(h, g, wr):
    t, d = h.shape
    tm = ROW_TILE
    row = lambda i: (i, 0)
    return pl.pallas_call(
        _router_kernel,
        grid=(t // tm,),
        in_specs=[pl.BlockSpec((tm, d), row), _resident((1, d)), _resident(wr.shape)],
        out_specs=[pl.BlockSpec((tm, d), row), pl.BlockSpec((tm, LANE), row), pl.BlockSpec((tm, LANE), row),
                   pl.BlockSpec((SUBLANE, LANE), lambda i: (0, 0))],
        out_shape=[jax.ShapeDtypeStruct((t, d), F32), jax.ShapeDtypeStruct((t, LANE), jnp.int32),
                   jax.ShapeDtypeStruct((t, LANE), F32), jax.ShapeDtypeStruct((SUBLANE, LANE), F32)],
        scratch_shapes=[pltpu.VMEM((1, LANE), F32)],
        compiler_params=_params("arbitrary"),
        name="router",
    )(h, g, wr)


def _row_copy(src, s, dst, d, sem):
    return pltpu.make_async_copy(src.at[pl.ds(s, 1)], dst.at[pl.ds(d, 1)], sem)


def _dispatch_kernel(pos_ref, hn_ref, xs_in_hbm, xs_hbm, sem):
    del xs_in_hbm
    i = pl.program_id(0)
    tm = hn_ref.shape[0]

    def issue(r, carry):
        _row_copy(hn_ref, r, xs_hbm, pos_ref[i, r], sem).start()
        _row_copy(hn_ref, r, xs_hbm, pos_ref[i, tm + r], sem).start()
        return carry

    lax.fori_loop(0, tm, issue, 0)
    for _ in range(2):
        pltpu.make_async_copy(hn_ref, xs_hbm.at[pl.ds(0, tm)], sem).wait()


def _dispatch(pos, hn, xs_zero):
    nt = pos.shape[0]
    t, d = hn.shape
    return pl.pallas_call(
        _dispatch_kernel,
        grid_spec=pltpu.PrefetchScalarGridSpec(
            num_scalar_prefetch=1, grid=(nt,),
            in_specs=[pl.BlockSpec((t // nt, d), lambda i, p: (i, 0)), pl.BlockSpec(memory_space=pl.ANY)],
            out_specs=pl.BlockSpec(memory_space=pl.ANY),
            scratch_shapes=[pltpu.SemaphoreType.DMA(())]),
        out_shape=jax.ShapeDtypeStruct(xs_zero.shape, xs_zero.dtype),
        input_output_aliases={2: 0},
        compiler_params=pltpu.CompilerParams(dimension_semantics=("arbitrary",), has_side_effects=True),
        name="dispatch",
    )(pos, hn, xs_zero)


def _experts_kernel(te_ref, na_ref, xs_ref, wg_ref, wu_ref, wd_ref, ys_ref, xb_ref, acc_ref):
    del te_ref
    i = pl.program_id(0)
    f = pl.program_id(1)

    @pl.when(f == 0)
    def _():
        xb_ref[...] = xs_ref[...].astype(BF16)
        acc_ref[...] = jnp.zeros_like(acc_ref)

    @pl.when(i < na_ref[0])
    def _():
        xb = xb_ref[...]
        act = jax.nn.silu(jnp.dot(xb, wg_ref[...], preferred_element_type=F32))
        act = act * jnp.dot(xb, wu_ref[...], preferred_element_type=F32)
        acc_ref[...] += jnp.dot(act.astype(BF16), wd_ref[...], preferred_element_type=F32)

    @pl.when(f == pl.num_programs(1) - 1)
    def _():
        ys_ref[...] = acc_ref[...]


def _experts(tile_expert, n_active, xs, wg, wu, wd):
    nr, d = xs.shape
    tm = MOE_TILE
    ff = wg.shape[2]
    tf = MOE_FF_TILE if ff % MOE_FF_TILE == 0 else ff
    return pl.pallas_call(
        _experts_kernel,
        grid_spec=pltpu.PrefetchScalarGridSpec(
            num_scalar_prefetch=2, grid=(nr // tm, ff // tf),
            in_specs=[pl.BlockSpec((tm, d), lambda i, f, te, na: (i, 0)),
                      pl.BlockSpec((None, d, tf), lambda i, f, te, na: (te[i], 0, f)),
                      pl.BlockSpec((None, d, tf), lambda i, f, te, na: (te[i], 0, f)),
                      pl.BlockSpec((None, tf, d), lambda i, f, te, na: (te[i], f, 0))],
            out_specs=pl.BlockSpec((tm, d), lambda i, f, te, na: (i, 0)),
            scratch_shapes=[pltpu.VMEM((tm, d), BF16), pltpu.VMEM((tm, d), F32)]),
        out_shape=jax.ShapeDtypeStruct((nr, d), F32),
        compiler_params=_params("arbitrary", "arbitrary"),
        name="experts",
    )(tile_expert, n_active, xs, wg, wu, wd)


def _combine_kernel(pos_ref, h_ref, wts_ref, gfin_ref, ys_hbm, o_ref, buf_ref, sem, *, final):
    i = pl.program_id(0)
    tm = h_ref.shape[0]

    def issue(r, carry):
        _row_copy(ys_hbm, pos_ref[i, r], buf_ref, r, sem).start()
        _row_copy(ys_hbm, pos_ref[i, tm + r], buf_ref, tm + r, sem).start()
        return carry

    lax.fori_loop(0, tm, issue, 0)
    pltpu.make_async_copy(ys_hbm.at[pl.ds(0, 2 * tm)], buf_ref, sem).wait()
    wts = wts_ref[...]
    out = h_ref[...] + wts[:, 0:1] * buf_ref[0:tm, :] + wts[:, 1:2] * buf_ref[tm:2 * tm, :]
    if final:
        out = _rms(out, gfin_ref[...])
    o_ref[...] = out


def _combine(pos, h, wts, gfin, ys, final):
    t, d = h.shape
    nt = pos.shape[0]
    tm = t // nt
    row = lambda i, p: (i, 0)
    return pl.pallas_call(
        functools.partial(_combine_kernel, final=final),
        grid_spec=pltpu.PrefetchScalarGridSpec(
            num_scalar_prefetch=1, grid=(nt,),
            in_specs=[pl.BlockSpec((tm, d), row), pl.BlockSpec((tm, LANE), row),
                      pl.BlockSpec((1, d), lambda i, p: (0, 0)), pl.BlockSpec(memory_space=pl.ANY)],
            out_specs=pl.BlockSpec((tm, d), row),
            scratch_shapes=[pltpu.VMEM((2 * tm, d), F32), pltpu.SemaphoreType.DMA(())]),
        out_shape=jax.ShapeDtypeStruct((t, d), F32),
        compiler_params=_params("arbitrary"),
        name="combine",
    )(pos, h, wts, gfin, ys)


def _moe(h, g, router_w, wg, wu, wd, gfin, final):
    t, d = h.shape
    tm = MOE_TILE
    wr = jnp.pad(router_w, ((0, 0), (0, LANE - N_EXPERTS)))
    hn, idx, wts, cnt = _router(h, g, wr)
    counts = cnt[0, :N_EXPERTS].astype(jnp.int32)
    padded = ((counts + tm - 1) // tm) * tm
    ends = jnp.cumsum(padded)
    offs = ends - padded
    n_tiles = (2 * t + N_EXPERTS * (tm - 1)) // tm
    pos1 = jnp.take(offs, idx[:, 0]) + idx[:, 2]
    pos2 = jnp.take(offs, idx[:, 1]) + idx[:, 3]
    pos = jnp.concatenate([pos1.reshape(-1, ROW_TILE), pos2.reshape(-1, ROW_TILE)], axis=1)
    tile_start = jnp.arange(n_tiles, dtype=jnp.int32) * tm
    tile_expert = jnp.minimum(jnp.sum(tile_start[:, None] >= ends[None, :], axis=1), N_EXPERTS - 1).astype(jnp.int32)
    n_active = (ends[-1:] // tm).astype(jnp.int32)
    xs = _dispatch(pos, hn, jnp.zeros((n_tiles * tm, d), F32))
    ys = _experts(tile_expert, n_active, xs, wg, wu, wd)
    return _combine(pos, h, wts, gfin, ys, final)


def _block_diag(w):
    n, a, b = w.shape
    eye = jnp.eye(n, dtype=w.dtype)
    return jnp.einsum('nij,nm->nimj', w, eye).reshape(n * a, n * b)


def kernel(x, meta_tokens, g_mix, w_in, b_forget, b_gate, conv_w, conv_b, lru_wa, lru_ba, lru_wx, lru_bx, lru_lambda, w_fox_o, w_sb_o, w_lru_o, w_out, g_ffn, ffn_w_gate, ffn_w_up, ffn_w_down, router_w, moe_w_gate, moe_w_up, moe_w_down, g_final):
    batch, seq, d = x.shape
    depth = g_mix.shape[0]
    l = seq + N_META
    lp = -(-l // SEQ_TILE) * SEQ_TILE
    assert (batch * lp) % ROW_TILE == 0 and lp % TIME_TILE == 0 and SEQ_TILE % KV_CHUNK == 0
    meta = jnp.broadcast_to(meta_tokens[None].astype(x.dtype), (batch, N_META, d))
    h = jnp.concatenate([meta, x, jnp.zeros((batch, lp - l, d), x.dtype)], axis=1).reshape(batch * lp, d)

    f0 = 3 * FOX_WIDTH
    s0 = f0 + FOX_HEADS
    x0 = s0 + 3 * SB_WIDTH
    g0 = x0 + 2 * LRU_WIDTH
    gfin = g_final.reshape(1, d)
    for layer in range(depth):
        wl = w_in[layer]
        wqkv = jnp.concatenate([wl[:, :f0], wl[:, s0:x0]], axis=1).astype(BF16)
        wlru = wl[:, x0:g0].astype(BF16)
        wf = jnp.pad(wl[:, f0:s0], ((0, 0), (0, LANE - FOX_HEADS))).astype(BF16)
        wgate = wl[:, g0:].astype(BF16)
        bf = jnp.pad(b_forget[layer], (0, LANE - FOX_HEADS)).reshape(1, LANE)

        qkv, lru, f = _inproj(h, g_mix[layer].reshape(1, d), wqkv, wlru, wf)
        yl, qaug, kaug = _lru(lru, f, conv_w[layer], conv_b[layer].reshape(1, -1),
                              _block_diag(lru_wa[layer]).astype(BF16), lru_ba[layer].reshape(1, -1),
                              _block_diag(lru_wx[layer]).astype(BF16), lru_bx[layer].reshape(1, -1),
                              lru_lambda[layer].reshape(1, -1), bf, batch)
        yf = _fox(qkv, qaug, kaug, batch)
        ys = _sb(qkv, batch)
        h = _merge(h, g_mix[layer].reshape(1, d), wgate, b_gate[layer].reshape(1, -1), yf, ys, yl,
                   w_fox_o[layer].astype(BF16), w_sb_o[layer].astype(BF16), w_lru_o[layer].astype(BF16),
                   w_out[layer].astype(BF16))

        final = layer == depth - 1
        j = layer // 2
        gf = g_ffn[layer].reshape(1, d)
        if layer % 2 == 0:
            h = _ffn(h, gf, ffn_w_gate[j].astype(BF16), ffn_w_up[j].astype(BF16), ffn_w_down[j].astype(BF16),
                     gfin, final)
        else:
            h = _moe(h, gf, router_w[j], moe_w_gate[j].astype(BF16), moe_w_up[j].astype(BF16),
                     moe_w_down[j].astype(BF16), gfin, final)
    return h.reshape(batch, lp, d)[:, N_META:l]
```

```python
import functools

import jax
import jax.numpy as jnp
from jax import lax
from jax.experimental import pallas as pl
from jax.experimental.pallas import tpu as pltpu

N_META = 16
HEAD_DIM = 128
FOX_HEADS = 4
SB_HEADS = 4
FOX_WIDTH = FOX_HEADS * HEAD_DIM
SB_WIDTH = SB_HEADS * HEAD_DIM
LRU_WIDTH = 512
LRU_BLOCKS = 8
LRU_C = 8.0
CONV_WIDTH = 4
N_BRANCH = 3
N_EXPERTS = 8
RMS_EPS = 1e-6
NEG = -1e30
SOFTPLUS_LINEAR = 40.0
COL_F = 3 * FOX_WIDTH
COL_SB = COL_F + FOX_HEADS
COL_LRU = COL_SB + 3 * SB_WIDTH
COL_GATE = COL_LRU + 2 * LRU_WIDTH

LANE = 128
SUBLANE = 8
SEQ_TILE = 768
KV_CHUNK = 256
ROW_TILE = 512
TIME_TILE = 256
MOE_TILE = 512
MOE_FF_TILE = 1792
ISSUE_UNROLL = 8
VMEM_LIMIT = 56 * 1024 * 1024
GATE_SHIFT = COL_GATE % LANE

F32 = jnp.float32
BF16 = jnp.bfloat16


def _params(*sem):
    return pltpu.CompilerParams(dimension_semantics=sem, vmem_limit_bytes=VMEM_LIMIT)


def _resident(shape):
    return pl.BlockSpec(shape, lambda *_: (0,) * len(shape), pipeline_mode=pl.Buffered(1))


def _rms(x, g):
    return x * lax.rsqrt(jnp.mean(x * x, axis=-1, keepdims=True) + RMS_EPS) * g


def _softplus(x):
    return jnp.maximum(x, 0.0) + jnp.log1p(jnp.exp(-jnp.abs(x)))


def _softplus_scores(x):
    return jnp.where(x > SOFTPLUS_LINEAR, x, jnp.log(1.0 + jnp.exp(x)))


def _inproj_kernel(h_ref, g_ref, w32_ref, qkv_ref, lru_ref, f_ref, wqkv_ref, wlru_ref, wf_ref):
    @pl.when(pl.program_id(0) == 0)
    def _():
        wqkv_ref[:, :COL_F] = w32_ref[:, :COL_F].astype(BF16)
        for c in range(3):
            src = slice(COL_SB + c * SB_WIDTH, COL_SB + (c + 1) * SB_WIDTH)
            wqkv_ref[:, COL_F + c * SB_WIDTH:COL_F + (c + 1) * SB_WIDTH] = w32_ref[:, src].astype(BF16)
        wlru_ref[...] = w32_ref[:, COL_LRU:COL_GATE].astype(BF16)
        wf_ref[...] = w32_ref[:, COL_F:COL_F + LANE].astype(BF16)

    xn = _rms(h_ref[...], g_ref[...]).astype(BF16)
    scale = HEAD_DIM ** -0.5
    for c in range(6):
        cols = slice(c * FOX_WIDTH, (c + 1) * FOX_WIDTH)
        r = jnp.dot(xn, wqkv_ref[:, cols], preferred_element_type=F32)
        if c % 3 == 0:
            r = r * scale
        qkv_ref[:, cols] = r.astype(BF16)
    lru_ref[...] = jnp.dot(xn, wlru_ref[...], preferred_element_type=F32)
    f_ref[...] = jnp.dot(xn, wf_ref[...], preferred_element_type=F32)


def _inproj(h, g, w_in, layer):
    t, d = h.shape
    tm = ROW_TILE
    nqkv = 3 * (FOX_WIDTH + SB_WIDTH)
    nlru = 2 * LRU_WIDTH
    wblock = -(-COL_GATE // LANE) * LANE
    return pl.pallas_call(
        _inproj_kernel,
        grid=(t // tm,),
        in_specs=[pl.BlockSpec((tm, d), lambda i: (i, 0)), _resident((1, d)),
                  pl.BlockSpec((None, d, wblock), lambda i: (layer, 0, 0), pipeline_mode=pl.Buffered(1))],
        out_specs=[pl.BlockSpec((tm, nqkv), lambda i: (i, 0)), pl.BlockSpec((tm, nlru), lambda i: (i, 0)),
                   pl.BlockSpec((tm, LANE), lambda i: (i, 0))],
        out_shape=[jax.ShapeDtypeStruct((t, nqkv), BF16), jax.ShapeDtypeStruct((t, nlru), F32),
                   jax.ShapeDtypeStruct((t, LANE), F32)],
        scratch_shapes=[pltpu.VMEM((d, nqkv), BF16), pltpu.VMEM((d, nlru), BF16), pltpu.VMEM((d, LANE), BF16)],
        compiler_params=_params("arbitrary"),
        name="inproj",
    )(h, g, w_in)


def _shift_rows(a, d, fill):
    rows = lax.broadcasted_iota(jnp.int32, a.shape, 0)
    return jnp.where(rows >= d, pltpu.roll(a, d, 0), fill)


def _lru_kernel(lru_ref, f_ref, cw_ref, cb_ref, wa_ref, ba_ref, wx_ref, bx_ref, lam_ref, bf_ref,
                y_ref, qa_ref, ka_ref, xx_ref, hprev_ref, cprev_ref):
    tt = lru_ref.shape[0]
    w = LRU_WIDTH

    @pl.when(pl.program_id(1) == 0)
    def _():
        xx_ref[0:SUBLANE, :] = jnp.zeros((SUBLANE, w), F32)
        hprev_ref[...] = jnp.zeros_like(hprev_ref)
        cprev_ref[...] = jnp.zeros_like(cprev_ref)

    x = lru_ref[:, :w]
    xx_ref[SUBLANE:SUBLANE + tt, :] = x
    u = cb_ref[...] + cw_ref[3:4, :] * x
    for k in range(CONV_WIDTH - 1):
        u = u + cw_ref[k:k + 1, :] * xx_ref[SUBLANE - 3 + k:SUBLANE - 3 + k + tt, :]
    xx_ref[0:SUBLANE, :] = x[tt - SUBLANE:, :]

    ub = u.astype(BF16)
    r = jax.nn.sigmoid(jnp.dot(ub, wa_ref[...], preferred_element_type=F32) + ba_ref[...])
    gi = jax.nn.sigmoid(jnp.dot(ub, wx_ref[...], preferred_element_type=F32) + bx_ref[...])
    log_a = (-LRU_C) * r * _softplus(-lam_ref[...])
    a = jnp.exp(log_a)
    b = jnp.sqrt(1.0 - a * a) * (gi * u)

    d = 1
    while d < tt:
        b = a * _shift_rows(b, d, 0.0) + b
        a = a * _shift_rows(a, d, 1.0)
        d *= 2
    h = b + a * hprev_ref[...]
    hprev_ref[...] = h[tt - 1:tt, :]
    y_ref[...] = (h * jax.nn.gelu(lru_ref[:, w:])).astype(y_ref.dtype)

    c = -_softplus(-(f_ref[...] + bf_ref[...]))
    d = 1
    while d < tt:
        c = c + _shift_rows(c, d, 0.0)
        d *= 2
    c = c + cprev_ref[...]
    cprev_ref[...] = c[tt - 1:tt, :]
    lane = lax.broadcasted_iota(jnp.int32, (tt, LANE), 1)
    term = jnp.where(lane >= 3, lane - 3, lane)
    for hd in range(FOX_HEADS):
        ch = jnp.broadcast_to(c[:, hd:hd + 1], (tt, LANE))
        hi = ch.astype(BF16).astype(F32)
        mid = (ch - hi).astype(BF16).astype(F32)
        lo = ((ch - hi) - mid).astype(BF16).astype(F32)
        c3 = jnp.where(term == 0, hi, jnp.where(term == 1, mid, lo))
        cols = slice(hd * LANE, (hd + 1) * LANE)
        qa_ref[:, cols] = jnp.where(lane < 3, c3, jnp.where(lane < 6, 1.0, 0.0)).astype(BF16)
        ka_ref[:, cols] = jnp.where(lane < 3, 1.0, jnp.where(lane < 6, -c3, 0.0)).astype(BF16)


def _lru(lru, f, cw, cb, wa, ba, wx, bx, lam, bf, batch):
    t = lru.shape[0]
    lp = t // batch
    tt = TIME_TILE
    nt = lp // tt
    w = LRU_WIDTH
    row = lambda b, i: (b * nt + i, 0)
    return pl.pallas_call(
        _lru_kernel,
        grid=(batch, nt),
        in_specs=[pl.BlockSpec((tt, 2 * w), row), pl.BlockSpec((tt, LANE), row),
                  _resident(cw.shape), _resident((1, w)), _resident((w, w)), _resident((1, w)),
                  _resident((w, w)), _resident((1, w)), _resident((1, w)), _resident((1, LANE))],
        out_specs=[pl.BlockSpec((tt, w), row), pl.BlockSpec((tt, FOX_WIDTH), row),
                   pl.BlockSpec((tt, FOX_WIDTH), row)],
        out_shape=[jax.ShapeDtypeStruct((t, w), BF16), jax.ShapeDtypeStruct((t, FOX_WIDTH), BF16),
                   jax.ShapeDtypeStruct((t, FOX_WIDTH), BF16)],
        scratch_shapes=[pltpu.VMEM((tt + SUBLANE, w), F32), pltpu.VMEM((1, w), F32), pltpu.VMEM((1, LANE), F32)],
        compiler_params=_params("parallel", "arbitrary"),
        name="lru",
    )(lru, f, cw, cb, wa, ba, wx, bx, lam, bf)


def _wide(a, width):
    return jnp.concatenate([a] * (width // LANE), axis=1)


def _fox_kernel(q_ref, qa_ref, k_ref, ka_ref, v_ref, o_ref, m_ref, acc_ref):
    i = pl.program_id(1)
    tq = q_ref.shape[0]
    tk = KV_CHUNK
    nd = tq // tk
    hd = HEAD_DIM
    ones = jnp.ones((tk, hd), BF16)
    m_ref[...] = jnp.full_like(m_ref, NEG)
    acc_ref[...] = jnp.zeros_like(acc_ref)

    def step(j, r0):
        off = pl.multiple_of(j * tk, tk)
        rows = slice(r0 or 0, tq)
        for h in range(FOX_HEADS):
            cols = slice(h * hd, (h + 1) * hd)
            q2 = jnp.concatenate([q_ref[rows, cols], qa_ref[rows, cols]], axis=1)
            k2 = jnp.concatenate([k_ref[pl.ds(off, tk), cols], ka_ref[pl.ds(off, tk), cols]], axis=1)
            v2 = jnp.concatenate([v_ref[pl.ds(off, tk), cols], ones], axis=1)
            s = lax.dot_general(q2, k2, (((1,), (1,)), ((), ())), preferred_element_type=F32)
            if r0 is not None:
                qpos = i * tq + r0 + lax.broadcasted_iota(jnp.int32, (tq - r0, 1), 0)
                kpos = j * tk + lax.broadcasted_iota(jnp.int32, (1, tk), 1)
                s = jnp.where(kpos <= qpos, s, NEG)
            m_prev = m_ref[h, rows, :]
            m_new = jnp.maximum(m_prev, jnp.max(s, axis=-1, keepdims=True))
            alpha = jnp.exp(m_prev - m_new)
            p = jnp.exp(s - _wide(m_new, tk))
            acc_ref[h, rows, :] = (_wide(alpha, 2 * hd) * acc_ref[h, rows, :]
                                   + jnp.dot(p.astype(BF16), v2, preferred_element_type=F32))
            m_ref[h, rows, :] = m_new

    def body(g, carry):
        for dd in range(nd):
            step(g * nd + dd, None)
        return carry

    lax.fori_loop(0, i, body, 0)
    for dd in range(nd):
        step(i * nd + dd, dd * tk)
    for h in range(FOX_HEADS):
        o_ref[:, h * hd:(h + 1) * hd] = (acc_ref[h, :, :hd] / acc_ref[h, :, hd:]).astype(o_ref.dtype)


def _seq_resident(lp, width, col):
    return pl.BlockSpec((lp, width), lambda b, i: (b, col), pipeline_mode=pl.Buffered(1))


def _fox(qkv, qaug, kaug, batch):
    t = qkv.shape[0]
    lp = t // batch
    tq = SEQ_TILE
    nq = lp // tq
    w = FOX_WIDTH
    qmap = lambda b, i: (b * nq + i, 0)
    return pl.pallas_call(
        _fox_kernel,
        grid=(batch, nq),
        in_specs=[pl.BlockSpec((tq, w), qmap), pl.BlockSpec((tq, w), qmap),
                  _seq_resident(lp, w, 1), _seq_resident(lp, w, 0), _seq_resident(lp, w, 2)],
        out_specs=pl.BlockSpec((tq, w), qmap),
        out_shape=jax.ShapeDtypeStruct((t, w), BF16),
        scratch_shapes=[pltpu.VMEM((FOX_HEADS, tq, LANE), F32), pltpu.VMEM((FOX_HEADS, tq, 2 * HEAD_DIM), F32)],
        compiler_params=_params("parallel", "arbitrary"),
        name="fox",
    )(qkv, qaug, qkv, kaug, qkv)


def _sb_kernel(q_ref, k_ref, v_ref, o_ref, carry_ref, acc_ref):
    i = pl.program_id(1)
    tq = q_ref.shape[0]
    tk = KV_CHUNK
    nd = tq // tk
    hd = HEAD_DIM
    rr = lax.broadcasted_iota(jnp.int32, (tk, tk), 0)
    cc = lax.broadcasted_iota(jnp.int32, (tk, tk), 1)
    later_mat = jnp.where(rr > cc, 1.0, 0.0).astype(BF16)
    carry_ref[...] = jnp.zeros_like(carry_ref)
    acc_ref[...] = jnp.zeros_like(acc_ref)

    def step(j, r0):
        off = pl.multiple_of(j * tk, tk)
        rows = slice(r0 or 0, tq)
        for h in range(SB_HEADS):
            cols = slice(h * hd, (h + 1) * hd)
            z = lax.dot_general(q_ref[rows, cols], k_ref[pl.ds(off, tk), cols], (((1,), (1,)), ((), ())),
                                preferred_element_type=F32)
            sp = _softplus_scores(z)
            spm = sp
            if r0 is not None:
                qpos = i * tq + r0 + lax.broadcasted_iota(jnp.int32, (tq - r0, 1), 0)
                mask = (j * tk + lax.broadcasted_iota(jnp.int32, (1, tk), 1)) < qpos
                spm = jnp.where(mask, sp, 0.0)
            cum = (jnp.dot(spm.astype(BF16), later_mat, preferred_element_type=F32)
                   + _wide(carry_ref[h, rows, :], tk))
            a = jnp.exp((z - sp) - cum)
            if r0 is not None:
                a = jnp.where(mask, a, 0.0)
            acc_ref[h, rows, :] += jnp.dot(a.astype(BF16), v_ref[pl.ds(off, tk), cols], preferred_element_type=F32)
            carry_ref[h, rows, :] = jnp.broadcast_to(cum[:, 0:1] + spm[:, 0:1], (tq - rows.start, LANE))

    for dd in range(nd - 1, -1, -1):
        step(i * nd + dd, dd * tk)

    def body(it, carry):
        for dd in range(nd - 1, -1, -1):
            step((i - 1 - it) * nd + dd, None)
        return carry

    lax.fori_loop(0, i, body, 0)
    for h in range(SB_HEADS):
        o_ref[:, h * hd:(h + 1) * hd] = acc_ref[h].astype(o_ref.dtype)


def _sb(qkv, batch):
    t = qkv.shape[0]
    lp = t // batch
    tq = SEQ_TILE
    nq = lp // tq
    w = SB_WIDTH
    base = 3 * FOX_WIDTH // w
    return pl.pallas_call(
        _sb_kernel,
        grid=(batch, nq),
        in_specs=[pl.BlockSpec((tq, w), lambda b, i: (b * nq + i, base)),
                  _seq_resident(lp, w, base + 1), _seq_resident(lp, w, base + 2)],
        out_specs=pl.BlockSpec((tq, w), lambda b, i: (b * nq + i, 0)),
        out_shape=jax.ShapeDtypeStruct((t, w), BF16),
        scratch_shapes=[pltpu.VMEM((SB_HEADS, tq, LANE), F32), pltpu.VMEM((SB_HEADS, tq, HEAD_DIM), F32)],
        compiler_params=_params("parallel", "arbitrary"),
        name="stickbreak",
    )(qkv, qkv, qkv)


def _merge_kernel(h_ref, g_ref, wg0_ref, wg1_ref, wg2_ref, wgt_ref, bgate_ref, yf_ref, ys_ref, yl_ref,
                  wfo_ref, wso_ref, wlo_ref, wout_ref, o_ref, wgate_ref):
    @pl.when(pl.program_id(0) == 0)
    def _():
        blocks = (wg0_ref, wg1_ref, wg2_ref, wgt_ref)
        dd = h_ref.shape[1]
        for br in range(N_BRANCH):
            both = jnp.concatenate([blocks[br][...], blocks[br + 1][:, :LANE]], axis=1)
            wgate_ref[:, br * dd:(br + 1) * dd] = both[:, GATE_SHIFT:GATE_SHIFT + dd].astype(BF16)

    h = h_ref[...]
    d = h.shape[1]
    xn = _rms(h, g_ref[...]).astype(BF16)
    merged = None
    for br, (y_ref, w_ref) in enumerate(((yf_ref, wfo_ref), (ys_ref, wso_ref), (yl_ref, wlo_ref))):
        cols = slice(br * d, (br + 1) * d)
        gate = jax.nn.sigmoid(jnp.dot(xn, wgate_ref[:, cols], preferred_element_type=F32) + bgate_ref[:, cols])
        term = gate * jnp.dot(y_ref[...], w_ref[...], preferred_element_type=F32)
        merged = term if merged is None else merged + term
    o_ref[...] = h + jnp.dot(merged.astype(BF16), wout_ref[...], preferred_element_type=F32)


def _merge(h, g, w_in, layer, bgate, yf, ys, yl, wfo, wso, wlo, wout):
    t, d = h.shape
    tm = ROW_TILE
    row = lambda i: (i, 0)
    first = COL_GATE - GATE_SHIFT
    assert first % d == 0
    wspec = lambda width, blk: pl.BlockSpec((None, d, width), lambda i: (layer, 0, blk),
                                            pipeline_mode=pl.Buffered(1))
    return pl.pallas_call(
        _merge_kernel,
        grid=(t // tm,),
        in_specs=[pl.BlockSpec((tm, d), row), _resident((1, d)),
                  wspec(d, first // d), wspec(d, first // d + 1), wspec(d, first // d + 2),
                  wspec(LANE, (first + N_BRANCH * d) // LANE), _resident(bgate.shape),
                  pl.BlockSpec((tm, yf.shape[1]), row), pl.BlockSpec((tm, ys.shape[1]), row),
                  pl.BlockSpec((tm, yl.shape[1]), row),
                  _resident(wfo.shape), _resident(wso.shape), _resident(wlo.shape), _resident(wout.shape)],
        out_specs=pl.BlockSpec((tm, d), row),
        out_shape=jax.ShapeDtypeStruct((t, d), F32),
        scratch_shapes=[pltpu.VMEM((d, N_BRANCH * d), BF16)],
        compiler_params=_params("arbitrary"),
        name="merge",
    )(h, g, w_in, w_in, w_in, w_in, bgate, yf, ys, yl, wfo, wso, wlo, wout)


def _ffn_chunk(ff):
    best = LANE
    for c in range(LANE, 1536 + 1, LANE):
        if ff % c == 0:
            best = c
    return best


def _ffn_kernel(h_ref, g_ref, wg_ref, wu_ref, wd_ref, gfin_ref, o_ref, *, final):
    h = h_ref[...]
    hn = _rms(h, g_ref[...]).astype(BF16)
    ff = wg_ref.shape[1]
    fc = _ffn_chunk(ff)
    out = h
    for c in range(ff // fc):
        cols = slice(c * fc, (c + 1) * fc)
        act = jax.nn.silu(jnp.dot(hn, wg_ref[:, cols], preferred_element_type=F32))
        act = act * jnp.dot(hn, wu_ref[:, cols], preferred_element_type=F32)
        out = out + jnp.dot(act.astype(BF16), wd_ref[cols, :], preferred_element_type=F32)
    if final:
        out = _rms(out, gfin_ref[...])
    o_ref[...] = out


def _ffn(h, g, wg, wu, wd, gfin, final):
    t, d = h.shape
    tm = ROW_TILE
    row = lambda i: (i, 0)
    return pl.pallas_call(
        functools.partial(_ffn_kernel, final=final),
        grid=(t // tm,),
        in_specs=[pl.BlockSpec((tm, d), row), _resident((1, d)), _resident(wg.shape), _resident(wu.shape),
                  _resident(wd.shape), _resident((1, d))],
        out_specs=pl.BlockSpec((tm, d), row),
        out_shape=jax.ShapeDtypeStruct((t, d), F32),
        compiler_params=_params("parallel"),
        name="ffn",
    )(h, g, wg, wu, wd, gfin)


def _router_kernel(h_ref, g_ref, wr_ref, hn_ref, idx_ref, wts_ref, cnt_ref, run_ref):
    tm = h_ref.shape[0]

    @pl.when(pl.program_id(0) == 0)
    def _():
        run_ref[...] = jnp.zeros_like(run_ref)

    hn = _rms(h_ref[...], g_ref[...])
    hn_ref[...] = hn
    logits = jnp.dot(hn, wr_ref[...], preferred_element_type=F32, precision=lax.Precision.HIGHEST)
    lane = lax.broadcasted_iota(jnp.int32, (tm, LANE), 1)
    lg = jnp.where(lane < N_EXPERTS, logits, -jnp.inf)
    v1 = jnp.max(lg, axis=-1, keepdims=True)
    i1 = jnp.min(jnp.where(lg == v1, lane, LANE), axis=-1, keepdims=True)
    lg2 = jnp.where(lane == i1, -jnp.inf, lg)
    v2 = jnp.max(lg2, axis=-1, keepdims=True)
    i2 = jnp.min(jnp.where(lg2 == v2, lane, LANE), axis=-1, keepdims=True)
    e = jnp.exp(v2 - v1)
    w1 = 1.0 / (1.0 + e)
    w2 = e * w1
    sel1 = lane == i1
    sel2 = lane == i2
    onehot = jnp.where(sel1, 1.0, 0.0) + jnp.where(sel2, 1.0, 0.0)
    rr = lax.broadcasted_iota(jnp.int32, (tm, tm), 0)
    cc = lax.broadcasted_iota(jnp.int32, (tm, tm), 1)
    before = jnp.where(cc < rr, 1.0, 0.0).astype(BF16)
    excl = jnp.dot(before, onehot.astype(BF16), preferred_element_type=F32) + run_ref[...]
    r1 = jnp.sum(jnp.where(sel1, excl, 0.0), axis=-1, keepdims=True)
    r2 = jnp.sum(jnp.where(sel2, excl, 0.0), axis=-1, keepdims=True)
    run = run_ref[...] + jnp.sum(onehot, axis=0, keepdims=True)
    run_ref[...] = run
    idx = jnp.where(lane == 0, i1, jnp.where(lane == 1, i2, 0))
    rank = jnp.where(lane == 2, r1, jnp.where(lane == 3, r2, 0.0)).astype(jnp.int32)
    idx_ref[...] = idx + rank
    wts_ref[...] = jnp.where(lane == 0, w1, jnp.where(lane == 1, w2, 0.0))
    cnt_ref[...] = jnp.broadcast_to(run, cnt_ref.shape)


def _router(h, g, wr):
    t, d = h.shape
    tm = ROW_TILE
    row = lambda i: (i, 0)
    return pl.pallas_call(
        _router_kernel,
        grid=(t // tm,),
        in_specs=[pl.BlockSpec((tm, d), row), _resident((1, d)), _resident(wr.shape)],
        out_specs=[pl.BlockSpec((tm, d), row), pl.BlockSpec((tm, LANE), row), pl.BlockSpec((tm, LANE), row),
                   pl.BlockSpec((SUBLANE, LANE), lambda i: (0, 0))],
        out_shape=[jax.ShapeDtypeStruct((t, d), F32), jax.ShapeDtypeStruct((t, LANE), jnp.int32),
                   jax.ShapeDtypeStruct((t, LANE), F32), jax.ShapeDtypeStruct((SUBLANE, LANE), F32)],
        scratch_shapes=[pltpu.VMEM((1, LANE), F32)],
        compiler_params=_params("arbitrary"),
        name="router",
    )(h, g, wr)


def _row_copy(src, s, dst, d, sem):
    return pltpu.make_async_copy(src.at[pl.ds(s, 1)], dst.at[pl.ds(d, 1)], sem)


def _dispatch_kernel(pos_ref, hn_ref, xs_in_hbm, xs_hbm, sem):
    del xs_in_hbm
    i = pl.program_id(0)
    tm = hn_ref.shape[0]

    def issue(g, carry):
        for u in range(ISSUE_UNROLL):
            r = g * ISSUE_UNROLL + u
            _row_copy(hn_ref, r, xs_hbm, pos_ref[i, r], sem).start()
            _row_copy(hn_ref, r, xs_hbm, pos_ref[i, tm + r], sem).start()
        return carry

    lax.fori_loop(0, tm // ISSUE_UNROLL, issue, 0)
    for _ in range(2):
        pltpu.make_async_copy(hn_ref, xs_hbm.at[pl.ds(0, tm)], sem).wait()


def _dispatch(pos, hn, xs_zero):
    nt = pos.shape[0]
    t, d = hn.shape
    return pl.pallas_call(
        _dispatch_kernel,
        grid_spec=pltpu.PrefetchScalarGridSpec(
            num_scalar_prefetch=1, grid=(nt,),
            in_specs=[pl.BlockSpec((t // nt, d), lambda i, p: (i, 0)), pl.BlockSpec(memory_space=pl.ANY)],
            out_specs=pl.BlockSpec(memory_space=pl.ANY),
            scratch_shapes=[pltpu.SemaphoreType.DMA(())]),
        out_shape=jax.ShapeDtypeStruct(xs_zero.shape, xs_zero.dtype),
        input_output_aliases={2: 0},
        compiler_params=pltpu.CompilerParams(dimension_semantics=("arbitrary",), has_side_effects=True,
                                             disable_bounds_checks=True),
        name="dispatch",
    )(pos, hn, xs_zero)


def _experts_kernel(te_ref, na_ref, xs_ref, wg_ref, wu_ref, wd_ref, ys_ref, xb_ref, acc_ref):
    del te_ref
    i = pl.program_id(0)
    f = pl.program_id(1)

    @pl.when(f == 0)
    def _():
        xb_ref[...] = xs_ref[...].astype(BF16)
        acc_ref[...] = jnp.zeros_like(acc_ref)

    @pl.when(i < na_ref[0])
    def _():
        xb = xb_ref[...]
        act = jax.nn.silu(jnp.dot(xb, wg_ref[...], preferred_element_type=F32))
        act = act * jnp.dot(xb, wu_ref[...], preferred_element_type=F32)
        acc_ref[...] += jnp.dot(act.astype(BF16), wd_ref[...], preferred_element_type=F32)

    @pl.when(f == pl.num_programs(1) - 1)
    def _():
        ys_ref[...] = acc_ref[...]


def _experts(tile_expert, n_active, xs, wg, wu, wd):
    nr, d = xs.shape
    tm = MOE_TILE
    ff = wg.shape[2]
    tf = MOE_FF_TILE if ff % MOE_FF_TILE == 0 else ff
    return pl.pallas_call(
        _experts_kernel,
        grid_spec=pltpu.PrefetchScalarGridSpec(
            num_scalar_prefetch=2, grid=(nr // tm, ff // tf),
            in_specs=[pl.BlockSpec((tm, d), lambda i, f, te, na: (i, 0)),
                      pl.BlockSpec((None, d, tf), lambda i, f, te, na: (te[i], 0, f)),
                      pl.BlockSpec((None, d, tf), lambda i, f, te, na: (te[i], 0, f)),
                      pl.BlockSpec((None, tf, d), lambda i, f, te, na: (te[i], f, 0))],
            out_specs=pl.BlockSpec((tm, d), lambda i, f, te, na: (i, 0)),
            scratch_shapes=[pltpu.VMEM((tm, d), BF16), pltpu.VMEM((tm, d), F32)]),
        out_shape=jax.ShapeDtypeStruct((nr, d), F32),
        compiler_params=_params("arbitrary", "arbitrary"),
        name="experts",
    )(tile_expert, n_active, xs, wg, wu, wd)


def _combine_kernel(pos_ref, h_ref, wts_ref, gfin_ref, ys_hbm, o_ref, buf_ref, sem, *, final):
    i = pl.program_id(0)
    tm = h_ref.shape[0]

    def issue(g, carry):
        for u in range(ISSUE_UNROLL):
            r = g * ISSUE_UNROLL + u
            _row_copy(ys_hbm, pos_ref[i, r], buf_ref, r, sem).start()
            _row_copy(ys_hbm, pos_ref[i, tm + r], buf_ref, tm + r, sem).start()
        return carry

    lax.fori_loop(0, tm // ISSUE_UNROLL, issue, 0)
    pltpu.make_async_copy(ys_hbm.at[pl.ds(0, 2 * tm)], buf_ref, sem).wait()
    wts = wts_ref[...]
    out = h_ref[...] + wts[:, 0:1] * buf_ref[0:tm, :] + wts[:, 1:2] * buf_ref[tm:2 * tm, :]
    if final:
        out = _rms(out, gfin_ref[...])
    o_ref[...] = out


def _combine(pos, h, wts, gfin, ys, final):
    t, d = h.shape
    nt = pos.shape[0]
    tm = t // nt
    row = lambda i, p: (i, 0)
    return pl.pallas_call(
        functools.partial(_combine_kernel, final=final),
        grid_spec=pltpu.PrefetchScalarGridSpec(
            num_scalar_prefetch=1, grid=(nt,),
            in_specs=[pl.BlockSpec((tm, d), row), pl.BlockSpec((tm, LANE), row),
                      pl.BlockSpec((1, d), lambda i, p: (0, 0)), pl.BlockSpec(memory_space=pl.ANY)],
            out_specs=pl.BlockSpec((tm, d), row),
            scratch_shapes=[pltpu.VMEM((2 * tm, d), F32), pltpu.SemaphoreType.DMA(())]),
        out_shape=jax.ShapeDtypeStruct((t, d), F32),
        compiler_params=pltpu.CompilerParams(dimension_semantics=("arbitrary",), vmem_limit_bytes=VMEM_LIMIT,
                                             disable_bounds_checks=True),
        name="combine",
    )(pos, h, wts, gfin, ys)


def _moe(h, g, router_w, wg, wu, wd, gfin, final):
    t, d = h.shape
    tm = MOE_TILE
    wr = jnp.pad(router_w, ((0, 0), (0, LANE - N_EXPERTS)))
    hn, idx, wts, cnt = _router(h, g, wr)
    counts = cnt[0, :N_EXPERTS].astype(jnp.int32)
    padded = ((counts + tm - 1) // tm) * tm
    ends = jnp.cumsum(padded)
    offs = ends - padded
    n_tiles = (2 * t + N_EXPERTS * (tm - 1)) // tm
    pos1 = jnp.take(offs, idx[:, 0]) + idx[:, 2]
    pos2 = jnp.take(offs, idx[:, 1]) + idx[:, 3]
    pos = jnp.concatenate([pos1.reshape(-1, ROW_TILE), pos2.reshape(-1, ROW_TILE)], axis=1)
    tile_start = jnp.arange(n_tiles, dtype=jnp.int32) * tm
    tile_expert = jnp.minimum(jnp.sum(tile_start[:, None] >= ends[None, :], axis=1), N_EXPERTS - 1).astype(jnp.int32)
    n_active = (ends[-1:] // tm).astype(jnp.int32)
    xs = _dispatch(pos, hn, jnp.zeros((n_tiles * tm, d), F32))
    ys = _experts(tile_expert, n_active, xs, wg, wu, wd)
    return _combine(pos, h, wts, gfin, ys, final)


def _block_diag(w):
    n, a, b = w.shape
    eye = jnp.eye(n, dtype=w.dtype)
    return jnp.einsum('nij,nm->nimj', w, eye).reshape(n * a, n * b)


def kernel(x, meta_tokens, g_mix, w_in, b_forget, b_gate, conv_w, conv_b, lru_wa, lru_ba, lru_wx, lru_bx, lru_lambda, w_fox_o, w_sb_o, w_lru_o, w_out, g_ffn, ffn_w_gate, ffn_w_up, ffn_w_down, router_w, moe_w_gate, moe_w_up, moe_w_down, g_final):
    batch, seq, d = x.shape
    depth = g_mix.shape[0]
    l = seq + N_META
    lp = -(-l // SEQ_TILE) * SEQ_TILE
    assert (batch * lp) % ROW_TILE == 0 and lp % TIME_TILE == 0 and SEQ_TILE % KV_CHUNK == 0
    meta = jnp.broadcast_to(meta_tokens[None].astype(x.dtype), (batch, N_META, d))
    h = jnp.concatenate([meta, x, jnp.zeros((batch, lp - l, d), x.dtype)], axis=1).reshape(batch * lp, d)

    gfin = g_final.reshape(1, d)
    for layer in range(depth):
        bf = jnp.pad(b_forget[layer], (0, LANE - FOX_HEADS)).reshape(1, LANE)

        qkv, lru, f = _inproj(h, g_mix[layer].reshape(1, d), w_in, layer)
        yl, qaug, kaug = _lru(lru, f, conv_w[layer], conv_b[layer].reshape(1, -1),
                              _block_diag(lru_wa[layer]).astype(BF16), lru_ba[layer].reshape(1, -1),
                              _block_diag(lru_wx[layer]).astype(BF16), lru_bx[layer].reshape(1, -1),
                              lru_lambda[layer].reshape(1, -1), bf, batch)
        yf = _fox(qkv, qaug, kaug, batch)
        ys = _sb(qkv, batch)
        h = _merge(h, g_mix[layer].reshape(1, d), w_in, layer, b_gate[layer].reshape(1, -1), yf, ys, yl,
                   w_fox_o[layer].astype(BF16), w_sb_o[layer].astype(BF16), w_lru_o[layer].astype(BF16),
                   w_out[layer].astype(BF16))

        final = layer == depth - 1
        j = layer // 2
        gf = g_ffn[layer].reshape(1, d)
        if layer % 2 == 0:
            h = _ffn(h, gf, ffn_w_gate[j].astype(BF16), ffn_w_up[j].astype(BF16), ffn_w_down[j].astype(BF16),
                     gfin, final)
        else:
            h = _moe(h, gf, router_w[j], moe_w_gate[j].astype(BF16), moe_w_up[j].astype(BF16),
                     moe_w_down[j].astype(BF16), gfin, final)
    return h.reshape(batch, lp, d)[:, N_META:l]
```

```python
import functools

import jax
import jax.numpy as jnp
from jax import lax
from jax.experimental import pallas as pl
from jax.experimental.pallas import tpu as pltpu

N_META = 16
HEAD_DIM = 128
FOX_HEADS = 4
SB_HEADS = 4
FOX_WIDTH = FOX_HEADS * HEAD_DIM
SB_WIDTH = SB_HEADS * HEAD_DIM
LRU_WIDTH = 512
LRU_BLOCKS = 8
LRU_C = 8.0
CONV_WIDTH = 4
N_BRANCH = 3
N_EXPERTS = 8
RMS_EPS = 1e-6
NEG = -1e30
SOFTPLUS_LINEAR = 40.0
EXP_ZERO = 105.0
COL_F = 3 * FOX_WIDTH
COL_SB = COL_F + FOX_HEADS
COL_LRU = COL_SB + 3 * SB_WIDTH
COL_GATE = COL_LRU + 2 * LRU_WIDTH

LANE = 128
SUBLANE = 8
SEQ_TILE = 768
KV_CHUNK = 256
ROW_TILE = 512
TIME_TILE = 256
MOE_TILE = 512
MOE_FF_TILE = 1792
ISSUE_UNROLL = 8
VMEM_LIMIT = 56 * 1024 * 1024
GATE_SHIFT = COL_GATE % LANE

F32 = jnp.float32
BF16 = jnp.bfloat16


def _params(*sem):
    return pltpu.CompilerParams(dimension_semantics=sem, vmem_limit_bytes=VMEM_LIMIT)


def _resident(shape):
    return pl.BlockSpec(shape, lambda *_: (0,) * len(shape), pipeline_mode=pl.Buffered(1))


def _rms(x, g):
    return x * lax.rsqrt(jnp.mean(x * x, axis=-1, keepdims=True) + RMS_EPS) * g


def _softplus(x):
    return jnp.maximum(x, 0.0) + jnp.log1p(jnp.exp(-jnp.abs(x)))


def _softplus_scores(x):
    return jnp.where(x > SOFTPLUS_LINEAR, x, jnp.log(1.0 + jnp.exp(x)))


def _inproj_kernel(h_ref, g_ref, w32_ref, qkv_ref, lru_ref, f_ref, wqkv_ref, wlru_ref, wf_ref):
    @pl.when(pl.program_id(0) == 0)
    def _():
        wqkv_ref[:, :COL_F] = w32_ref[:, :COL_F].astype(BF16)
        for c in range(3):
            src = slice(COL_SB + c * SB_WIDTH, COL_SB + (c + 1) * SB_WIDTH)
            wqkv_ref[:, COL_F + c * SB_WIDTH:COL_F + (c + 1) * SB_WIDTH] = w32_ref[:, src].astype(BF16)
        wlru_ref[...] = w32_ref[:, COL_LRU:COL_GATE].astype(BF16)
        wf_ref[...] = w32_ref[:, COL_F:COL_F + LANE].astype(BF16)

    xn = _rms(h_ref[...], g_ref[...]).astype(BF16)
    scale = HEAD_DIM ** -0.5
    for c in range(6):
        cols = slice(c * FOX_WIDTH, (c + 1) * FOX_WIDTH)
        r = jnp.dot(xn, wqkv_ref[:, cols], preferred_element_type=F32)
        if c % 3 == 0:
            r = r * scale
        qkv_ref[:, cols] = r.astype(BF16)
    lru_ref[...] = jnp.dot(xn, wlru_ref[...], preferred_element_type=F32)
    f_ref[...] = jnp.dot(xn, wf_ref[...], preferred_element_type=F32)


def _inproj(h, g, w_in, layer):
    t, d = h.shape
    tm = ROW_TILE
    nqkv = 3 * (FOX_WIDTH + SB_WIDTH)
    nlru = 2 * LRU_WIDTH
    wblock = -(-COL_GATE // LANE) * LANE
    return pl.pallas_call(
        _inproj_kernel,
        grid=(t // tm,),
        in_specs=[pl.BlockSpec((tm, d), lambda i: (i, 0)), _resident((1, d)),
                  pl.BlockSpec((None, d, wblock), lambda i: (layer, 0, 0), pipeline_mode=pl.Buffered(1))],
        out_specs=[pl.BlockSpec((tm, nqkv), lambda i: (i, 0)), pl.BlockSpec((tm, nlru), lambda i: (i, 0)),
                   pl.BlockSpec((tm, LANE), lambda i: (i, 0))],
        out_shape=[jax.ShapeDtypeStruct((t, nqkv), BF16), jax.ShapeDtypeStruct((t, nlru), F32),
                   jax.ShapeDtypeStruct((t, LANE), F32)],
        scratch_shapes=[pltpu.VMEM((d, nqkv), BF16), pltpu.VMEM((d, nlru), BF16), pltpu.VMEM((d, LANE), BF16)],
        compiler_params=_params("arbitrary"),
        name="inproj",
    )(h, g, w_in)


def _shift_rows(a, d, fill):
    rows = lax.broadcasted_iota(jnp.int32, a.shape, 0)
    return jnp.where(rows >= d, pltpu.roll(a, d, 0), fill)


def _lru_kernel(lru_ref, f_ref, cw_ref, cb_ref, wa_ref, ba_ref, wx_ref, bx_ref, lam_ref, bf_ref,
                y_ref, qa_ref, ka_ref, xx_ref, hprev_ref, cprev_ref):
    tt = lru_ref.shape[0]
    w = LRU_WIDTH

    @pl.when(pl.program_id(1) == 0)
    def _():
        xx_ref[0:SUBLANE, :] = jnp.zeros((SUBLANE, w), F32)
        hprev_ref[...] = jnp.zeros_like(hprev_ref)
        cprev_ref[...] = jnp.zeros_like(cprev_ref)

    x = lru_ref[:, :w]
    xx_ref[SUBLANE:SUBLANE + tt, :] = x
    u = cb_ref[...] + cw_ref[3:4, :] * x
    for k in range(CONV_WIDTH - 1):
        u = u + cw_ref[k:k + 1, :] * xx_ref[SUBLANE - 3 + k:SUBLANE - 3 + k + tt, :]
    xx_ref[0:SUBLANE, :] = x[tt - SUBLANE:, :]

    ub = u.astype(BF16)
    r = jax.nn.sigmoid(jnp.dot(ub, wa_ref[...], preferred_element_type=F32) + ba_ref[...])
    gi = jax.nn.sigmoid(jnp.dot(ub, wx_ref[...], preferred_element_type=F32) + bx_ref[...])
    log_a = (-LRU_C) * r * _softplus(-lam_ref[...])
    a = jnp.exp(log_a)
    b = jnp.sqrt(1.0 - a * a) * (gi * u)

    d = 1
    while d < tt:
        b = a * _shift_rows(b, d, 0.0) + b
        a = a * _shift_rows(a, d, 1.0)
        d *= 2
    h = b + a * hprev_ref[...]
    hprev_ref[...] = h[tt - 1:tt, :]
    y_ref[...] = (h * jax.nn.gelu(lru_ref[:, w:])).astype(y_ref.dtype)

    c = -_softplus(-(f_ref[...] + bf_ref[...]))
    d = 1
    while d < tt:
        c = c + _shift_rows(c, d, 0.0)
        d *= 2
    c = c + cprev_ref[...]
    cprev_ref[...] = c[tt - 1:tt, :]
    lane = lax.broadcasted_iota(jnp.int32, (tt, LANE), 1)
    term = jnp.where(lane >= 3, lane - 3, lane)
    for hd in range(FOX_HEADS):
        ch = jnp.broadcast_to(c[:, hd:hd + 1], (tt, LANE))
        hi = ch.astype(BF16).astype(F32)
        mid = (ch - hi).astype(BF16).astype(F32)
        lo = ((ch - hi) - mid).astype(BF16).astype(F32)
        c3 = jnp.where(term == 0, hi, jnp.where(term == 1, mid, lo))
        cols = slice(hd * LANE, (hd + 1) * LANE)
        qa_ref[:, cols] = jnp.where(lane < 3, c3, jnp.where(lane < 6, 1.0, 0.0)).astype(BF16)
        ka_ref[:, cols] = jnp.where(lane < 3, 1.0, jnp.where(lane < 6, -c3, 0.0)).astype(BF16)


def _lru(lru, f, cw, cb, wa, ba, wx, bx, lam, bf, batch):
    t = lru.shape[0]
    lp = t // batch
    tt = TIME_TILE
    nt = lp // tt
    w = LRU_WIDTH
    row = lambda b, i: (b * nt + i, 0)
    return pl.pallas_call(
        _lru_kernel,
        grid=(batch, nt),
        in_specs=[pl.BlockSpec((tt, 2 * w), row), pl.BlockSpec((tt, LANE), row),
                  _resident(cw.shape), _resident((1, w)), _resident((w, w)), _resident((1, w)),
                  _resident((w, w)), _resident((1, w)), _resident((1, w)), _resident((1, LANE))],
        out_specs=[pl.BlockSpec((tt, w), row), pl.BlockSpec((tt, FOX_WIDTH), row),
                   pl.BlockSpec((tt, FOX_WIDTH), row)],
        out_shape=[jax.ShapeDtypeStruct((t, w), BF16), jax.ShapeDtypeStruct((t, FOX_WIDTH), BF16),
                   jax.ShapeDtypeStruct((t, FOX_WIDTH), BF16)],
        scratch_shapes=[pltpu.VMEM((tt + SUBLANE, w), F32), pltpu.VMEM((1, w), F32), pltpu.VMEM((1, LANE), F32)],
        compiler_params=_params("parallel", "arbitrary"),
        name="lru",
    )(lru, f, cw, cb, wa, ba, wx, bx, lam, bf)


def _wide(a, width):
    return jnp.concatenate([a] * (width // LANE), axis=1)


def _row_norm_max(x):
    x = x.astype(F32)
    return jnp.sqrt(jnp.max(jnp.sum(x * x, axis=1, keepdims=True)))


def _fox_kernel(q_ref, qa_ref, k_ref, ka_ref, v_ref, o_ref, m_ref, acc_ref, kn_ref):
    i = pl.program_id(1)
    tq = q_ref.shape[0]
    tk = KV_CHUNK
    nd = tq // tk
    hd = HEAD_DIM
    ones = jnp.ones((tk, hd), BF16)

    @pl.when(i == 0)
    def _():
        for h in range(FOX_HEADS):
            def blk(r, best, h=h):
                rows = pl.ds(pl.multiple_of(r * tq, tq), tq)
                return jnp.maximum(best, _row_norm_max(k_ref[rows, h * hd:(h + 1) * hd]))
            kn_ref[h] = lax.fori_loop(0, k_ref.shape[0] // tq, blk, jnp.float32(0.0))

    m_ref[...] = jnp.full_like(m_ref, NEG)
    acc_ref[...] = jnp.zeros_like(acc_ref)

    def step(j, r0):
        off = pl.multiple_of(j * tk, tk)
        rows = slice(r0 or 0, tq)
        for h in range(FOX_HEADS):
            cols = slice(h * hd, (h + 1) * hd)
            q2 = jnp.concatenate([q_ref[rows, cols], qa_ref[rows, cols]], axis=1)
            k2 = jnp.concatenate([k_ref[pl.ds(off, tk), cols], ka_ref[pl.ds(off, tk), cols]], axis=1)
            v2 = jnp.concatenate([v_ref[pl.ds(off, tk), cols], ones], axis=1)
            s = lax.dot_general(q2, k2, (((1,), (1,)), ((), ())), preferred_element_type=F32)
            if r0 is not None:
                qpos = i * tq + r0 + lax.broadcasted_iota(jnp.int32, (tq - r0, 1), 0)
                kpos = j * tk + lax.broadcasted_iota(jnp.int32, (1, tk), 1)
                s = jnp.where(kpos <= qpos, s, NEG)
            m_prev = m_ref[h, rows, :]
            m_new = jnp.maximum(m_prev, jnp.max(s, axis=-1, keepdims=True))
            alpha = jnp.exp(m_prev - m_new)
            p = jnp.exp(s - _wide(m_new, tk))
            acc_ref[h, rows, :] = (_wide(alpha, 2 * hd) * acc_ref[h, rows, :]
                                   + jnp.dot(p.astype(BF16), v2, preferred_element_type=F32))
            m_ref[h, rows, :] = m_new

    for dd in range(nd):
        step(i * nd + dd, dd * tk)

    qn = [_row_norm_max(q_ref[:, h * hd:(h + 1) * hd]) for h in range(FOX_HEADS)]

    def negligible(g):
        last = pl.ds(pl.multiple_of((g + 1) * tq - 2 * SUBLANE, 2 * SUBLANE), 2 * SUBLANE)
        done = None
        for h in range(FOX_HEADS):
            cols = slice(h * hd, (h + 1) * hd)
            k_last = ka_ref[last, cols][2 * SUBLANE - 1:, :].astype(F32)
            bias = jnp.sum(qa_ref[0:1, cols].astype(F32) * k_last)
            small = qn[h] * kn_ref[h] + bias < jnp.min(m_ref[h]) - EXP_ZERO
            done = small if done is None else jnp.logical_and(done, small)
        return done

    def more(state):
        g, done = state
        return jnp.logical_and(g >= 0, jnp.logical_not(done))

    def body(state):
        g, _ = state
        for dd in range(nd - 1, -1, -1):
            step(g * nd + dd, None)
        return g - 1, negligible(jnp.maximum(g - 1, 0))

    lax.while_loop(more, body, (i - 1, negligible(jnp.maximum(i - 1, 0))))
    for h in range(FOX_HEADS):
        o_ref[:, h * hd:(h + 1) * hd] = (acc_ref[h, :, :hd] / acc_ref[h, :, hd:]).astype(o_ref.dtype)


def _seq_resident(lp, width, col):
    return pl.BlockSpec((lp, width), lambda b, i: (b, col), pipeline_mode=pl.Buffered(1))


def _fox(qkv, qaug, kaug, batch):
    t = qkv.shape[0]
    lp = t // batch
    tq = SEQ_TILE
    nq = lp // tq
    w = FOX_WIDTH
    qmap = lambda b, i: (b * nq + i, 0)
    return pl.pallas_call(
        _fox_kernel,
        grid=(batch, nq),
        in_specs=[pl.BlockSpec((tq, w), qmap), pl.BlockSpec((tq, w), qmap),
                  _seq_resident(lp, w, 1), _seq_resident(lp, w, 0), _seq_resident(lp, w, 2)],
        out_specs=pl.BlockSpec((tq, w), qmap),
        out_shape=jax.ShapeDtypeStruct((t, w), BF16),
        scratch_shapes=[pltpu.VMEM((FOX_HEADS, tq, LANE), F32), pltpu.VMEM((FOX_HEADS, tq, 2 * HEAD_DIM), F32),
                        pltpu.SMEM((FOX_HEADS,), F32)],
        compiler_params=_params("arbitrary", "arbitrary"),
        name="fox",
    )(qkv, qaug, qkv, kaug, qkv)


def _sb_kernel(q_ref, k_ref, v_ref, o_ref, carry_ref, acc_ref):
    i = pl.program_id(1)
    tq = q_ref.shape[0]
    tk = KV_CHUNK
    nd = tq // tk
    hd = HEAD_DIM
    rr = lax.broadcasted_iota(jnp.int32, (tk, tk), 0)
    cc = lax.broadcasted_iota(jnp.int32, (tk, tk), 1)
    later_mat = jnp.where(rr > cc, 1.0, 0.0).astype(BF16)
    carry_ref[...] = jnp.zeros_like(carry_ref)
    acc_ref[...] = jnp.zeros_like(acc_ref)

    def step(j, r0):
        off = pl.multiple_of(j * tk, tk)
        rows = slice(r0 or 0, tq)
        for h in range(SB_HEADS):
            cols = slice(h * hd, (h + 1) * hd)
            z = lax.dot_general(q_ref[rows, cols], k_ref[pl.ds(off, tk), cols], (((1,), (1,)), ((), ())),
                                preferred_element_type=F32)
            sp = _softplus_scores(z)
            spm = sp
            if r0 is not None:
                qpos = i * tq + r0 + lax.broadcasted_iota(jnp.int32, (tq - r0, 1), 0)
                mask = (j * tk + lax.broadcasted_iota(jnp.int32, (1, tk), 1)) < qpos
                spm = jnp.where(mask, sp, 0.0)
            cum = (jnp.dot(spm.astype(BF16), later_mat, preferred_element_type=F32)
                   + _wide(carry_ref[h, rows, :], tk))
            a = jnp.exp((z - sp) - cum)
            if r0 is not None:
                a = jnp.where(mask, a, 0.0)
            acc_ref[h, rows, :] += jnp.dot(a.astype(BF16), v_ref[pl.ds(off, tk), cols], preferred_element_type=F32)
            carry_ref[h, rows, :] = jnp.broadcast_to(cum[:, 0:1] + spm[:, 0:1], (tq - rows.start, LANE))

    for dd in range(nd - 1, -1, -1):
        step(i * nd + dd, dd * tk)

    def least_carry():
        least = carry_ref[0]
        for h in range(1, SB_HEADS):
            least = jnp.minimum(least, carry_ref[h])
        return jnp.min(least)

    def more(state):
        it, least = state
        return jnp.logical_and(it < i, least < EXP_ZERO)

    def body(state):
        it, _ = state
        for dd in range(nd - 1, -1, -1):
            step((i - 1 - it) * nd + dd, None)
        return it + 1, least_carry()

    lax.while_loop(more, body, (jnp.int32(0), least_carry()))
    for h in range(SB_HEADS):
        o_ref[:, h * hd:(h + 1) * hd] = acc_ref[h].astype(o_ref.dtype)


def _sb(qkv, batch):
    t = qkv.shape[0]
    lp = t // batch
    tq = SEQ_TILE
    nq = lp // tq
    w = SB_WIDTH
    base = 3 * FOX_WIDTH // w
    return pl.pallas_call(
        _sb_kernel,
        grid=(batch, nq),
        in_specs=[pl.BlockSpec((tq, w), lambda b, i: (b * nq + i, base)),
                  _seq_resident(lp, w, base + 1), _seq_resident(lp, w, base + 2)],
        out_specs=pl.BlockSpec((tq, w), lambda b, i: (b * nq + i, 0)),
        out_shape=jax.ShapeDtypeStruct((t, w), BF16),
        scratch_shapes=[pltpu.VMEM((SB_HEADS, tq, LANE), F32), pltpu.VMEM((SB_HEADS, tq, HEAD_DIM), F32)],
        compiler_params=_params("parallel", "arbitrary"),
        name="stickbreak",
    )(qkv, qkv, qkv)


def _merge_kernel(h_ref, g_ref, wg0_ref, wg1_ref, wg2_ref, wgt_ref, bgate_ref, yf_ref, ys_ref, yl_ref,
                  wfo_ref, wso_ref, wlo_ref, wout_ref, o_ref, wgate_ref):
    @pl.when(pl.program_id(0) == 0)
    def _():
        blocks = (wg0_ref, wg1_ref, wg2_ref, wgt_ref)
        dd = h_ref.shape[1]
        for br in range(N_BRANCH):
            both = jnp.concatenate([blocks[br][...], blocks[br + 1][:, :LANE]], axis=1)
            wgate_ref[:, br * dd:(br + 1) * dd] = both[:, GATE_SHIFT:GATE_SHIFT + dd].astype(BF16)

    h = h_ref[...]
    d = h.shape[1]
    xn = _rms(h, g_ref[...]).astype(BF16)
    merged = None
    for br, (y_ref, w_ref) in enumerate(((yf_ref, wfo_ref), (ys_ref, wso_ref), (yl_ref, wlo_ref))):
        cols = slice(br * d, (br + 1) * d)
        gate = jax.nn.sigmoid(jnp.dot(xn, wgate_ref[:, cols], preferred_element_type=F32) + bgate_ref[:, cols])
        term = gate * jnp.dot(y_ref[...], w_ref[...], preferred_element_type=F32)
        merged = term if merged is None else merged + term
    o_ref[...] = h + jnp.dot(merged.astype(BF16), wout_ref[...], preferred_element_type=F32)


def _merge(h, g, w_in, layer, bgate, yf, ys, yl, wfo, wso, wlo, wout):
    t, d = h.shape
    tm = ROW_TILE
    row = lambda i: (i, 0)
    first = COL_GATE - GATE_SHIFT
    assert first % d == 0
    wspec = lambda width, blk: pl.BlockSpec((None, d, width), lambda i: (layer, 0, blk),
                                            pipeline_mode=pl.Buffered(1))
    return pl.pallas_call(
        _merge_kernel,
        grid=(t // tm,),
        in_specs=[pl.BlockSpec((tm, d), row), _resident((1, d)),
                  wspec(d, first // d), wspec(d, first // d + 1), wspec(d, first // d + 2),
                  wspec(LANE, (first + N_BRANCH * d) // LANE), _resident(bgate.shape),
                  pl.BlockSpec((tm, yf.shape[1]), row), pl.BlockSpec((tm, ys.shape[1]), row),
                  pl.BlockSpec((tm, yl.shape[1]), row),
                  _resident(wfo.shape), _resident(wso.shape), _resident(wlo.shape), _resident(wout.shape)],
        out_specs=pl.BlockSpec((tm, d), row),
        out_shape=jax.ShapeDtypeStruct((t, d), F32),
        scratch_shapes=[pltpu.VMEM((d, N_BRANCH * d), BF16)],
        compiler_params=_params("arbitrary"),
        name="merge",
    )(h, g, w_in, w_in, w_in, w_in, bgate, yf, ys, yl, wfo, wso, wlo, wout)


def _ffn_chunk(ff):
    best = LANE
    for c in range(LANE, 1536 + 1, LANE):
        if ff % c == 0:
            best = c
    return best


def _ffn_kernel(h_ref, g_ref, wg_ref, wu_ref, wd_ref, gfin_ref, o_ref, *, final):
    h = h_ref[...]
    hn = _rms(h, g_ref[...]).astype(BF16)
    ff = wg_ref.shape[1]
    fc = _ffn_chunk(ff)
    out = h
    for c in range(ff // fc):
        cols = slice(c * fc, (c + 1) * fc)
        act = jax.nn.silu(jnp.dot(hn, wg_ref[:, cols], preferred_element_type=F32))
        act = act * jnp.dot(hn, wu_ref[:, cols], preferred_element_type=F32)
        out = out + jnp.dot(act.astype(BF16), wd_ref[cols, :], preferred_element_type=F32)
    if final:
        out = _rms(out, gfin_ref[...])
    o_ref[...] = out


def _ffn(h, g, wg, wu, wd, gfin, final):
    t, d = h.shape
    tm = ROW_TILE
    row = lambda i: (i, 0)
    return pl.pallas_call(
        functools.partial(_ffn_kernel, final=final),
        grid=(t // tm,),
        in_specs=[pl.BlockSpec((tm, d), row), _resident((1, d)), _resident(wg.shape), _resident(wu.shape),
                  _resident(wd.shape), _resident((1, d))],
        out_specs=pl.BlockSpec((tm, d), row),
        out_shape=jax.ShapeDtypeStruct((t, d), F32),
        compiler_params=_params("parallel"),
        name="ffn",
    )(h, g, wg, wu, wd, gfin)


def _router_kernel(h_ref, g_ref, wr_ref, hn_ref, idx_ref, wts_ref, cnt_ref, run_ref):
    tm = h_ref.shape[0]

    @pl.when(pl.program_id(0) == 0)
    def _():
        run_ref[...] = jnp.zeros_like(run_ref)

    hn = _rms(h_ref[...], g_ref[...])
    hn_ref[...] = hn
    logits = jnp.dot(hn, wr_ref[...], preferred_element_type=F32, precision=lax.Precision.HIGHEST)
    lane = lax.broadcasted_iota(jnp.int32, (tm, LANE), 1)
    lg = jnp.where(lane < N_EXPERTS, logits, -jnp.inf)
    v1 = jnp.max(lg, axis=-1, keepdims=True)
    i1 = jnp.min(jnp.where(lg == v1, lane, LANE), axis=-1, keepdims=True)
    lg2 = jnp.where(lane == i1, -jnp.inf, lg)
    v2 = jnp.max(lg2, axis=-1, keepdims=True)
    i2 = jnp.min(jnp.where(lg2 == v2, lane, LANE), axis=-1, keepdims=True)
    e = jnp.exp(v2 - v1)
    w1 = 1.0 / (1.0 + e)
    w2 = e * w1
    sel1 = lane == i1
    sel2 = lane == i2
    onehot = jnp.where(sel1, 1.0, 0.0) + jnp.where(sel2, 1.0, 0.0)
    rr = lax.broadcasted_iota(jnp.int32, (tm, tm), 0)
    cc = lax.broadcasted_iota(jnp.int32, (tm, tm), 1)
    before = jnp.where(cc < rr, 1.0, 0.0).astype(BF16)
    excl = jnp.dot(before, onehot.astype(BF16), preferred_element_type=F32) + run_ref[...]
    r1 = jnp.sum(jnp.where(sel1, excl, 0.0), axis=-1, keepdims=True)
    r2 = jnp.sum(jnp.where(sel2, excl, 0.0), axis=-1, keepdims=True)
    run = run_ref[...] + jnp.sum(onehot, axis=0, keepdims=True)
    run_ref[...] = run
    idx = jnp.where(lane == 0, i1, jnp.where(lane == 1, i2, 0))
    rank = jnp.where(lane == 2, r1, jnp.where(lane == 3, r2, 0.0)).astype(jnp.int32)
    idx_ref[...] = idx + rank
    wts_ref[...] = jnp.where(lane == 0, w1, jnp.where(lane == 1, w2, 0.0))
    cnt_ref[...] = jnp.broadcast_to(run, cnt_ref.shape)


def _router(h, g, wr):
    t, d = h.shape
    tm = ROW_TILE
    row = lambda i: (i, 0)
    return pl.pallas_call(
        _router_kernel,
        grid=(t // tm,),
        in_specs=[pl.BlockSpec((tm, d), row), _resident((1, d)), _resident(wr.shape)],
        out_specs=[pl.BlockSpec((tm, d), row), pl.BlockSpec((tm, LANE), row), pl.BlockSpec((tm, LANE), row),
                   pl.BlockSpec((SUBLANE, LANE), lambda i: (0, 0))],
        out_shape=[jax.ShapeDtypeStruct((t, d), F32), jax.ShapeDtypeStruct((t, LANE), jnp.int32),
                   jax.ShapeDtypeStruct((t, LANE), F32), jax.ShapeDtypeStruct((SUBLANE, LANE), F32)],
        scratch_shapes=[pltpu.VMEM((1, LANE), F32)],
        compiler_params=_params("arbitrary"),
        name="router",
    )(h, g, wr)


def _row_copy(src, s, dst, d, sem):
    return pltpu.make_async_copy(src.at[pl.ds(s, 1)], dst.at[pl.ds(d, 1)], sem)


def _dispatch_kernel(pos_ref, hn_ref, xs_in_hbm, xs_hbm, sem):
    del xs_in_hbm
    i = pl.program_id(0)
    tm = hn_ref.shape[0]

    def issue(g, carry):
        for u in range(ISSUE_UNROLL):
            r = g * ISSUE_UNROLL + u
            _row_copy(hn_ref, r, xs_hbm, pos_ref[i, r], sem).start()
            _row_copy(hn_ref, r, xs_hbm, pos_ref[i, tm + r], sem).start()
        return carry

    lax.fori_loop(0, tm // ISSUE_UNROLL, issue, 0)
    for _ in range(2):
        pltpu.make_async_copy(hn_ref, xs_hbm.at[pl.ds(0, tm)], sem).wait()


def _dispatch(pos, hn, xs_zero):
    nt = pos.shape[0]
    t, d = hn.shape
    return pl.pallas_call(
        _dispatch_kernel,
        grid_spec=pltpu.PrefetchScalarGridSpec(
            num_scalar_prefetch=1, grid=(nt,),
            in_specs=[pl.BlockSpec((t // nt, d), lambda i, p: (i, 0)), pl.BlockSpec(memory_space=pl.ANY)],
            out_specs=pl.BlockSpec(memory_space=pl.ANY),
            scratch_shapes=[pltpu.SemaphoreType.DMA(())]),
        out_shape=jax.ShapeDtypeStruct(xs_zero.shape, xs_zero.dtype),
        input_output_aliases={2: 0},
        compiler_params=pltpu.CompilerParams(dimension_semantics=("arbitrary",), has_side_effects=True,
                                             disable_bounds_checks=True),
        name="dispatch",
    )(pos, hn, xs_zero)


def _experts_kernel(te_ref, na_ref, xs_ref, wg_ref, wu_ref, wd_ref, ys_ref, xb_ref, acc_ref):
    del te_ref
    i = pl.program_id(0)
    f = pl.program_id(1)

    @pl.when(f == 0)
    def _():
        xb_ref[...] = xs_ref[...].astype(BF16)
        acc_ref[...] = jnp.zeros_like(acc_ref)

    @pl.when(i < na_ref[0])
    def _():
        xb = xb_ref[...]
        act = jax.nn.silu(jnp.dot(xb, wg_ref[...], preferred_element_type=F32))
        act = act * jnp.dot(xb, wu_ref[...], preferred_element_type=F32)
        acc_ref[...] += jnp.dot(act.astype(BF16), wd_ref[...], preferred_element_type=F32)

    @pl.when(f == pl.num_programs(1) - 1)
    def _():
        ys_ref[...] = acc_ref[...]


def _experts(tile_expert, n_active, xs, wg, wu, wd):
    nr, d = xs.shape
    tm = MOE_TILE
    ff = wg.shape[2]
    tf = MOE_FF_TILE if ff % MOE_FF_TILE == 0 else ff
    return pl.pallas_call(
        _experts_kernel,
        grid_spec=pltpu.PrefetchScalarGridSpec(
            num_scalar_prefetch=2, grid=(nr // tm, ff // tf),
            in_specs=[pl.BlockSpec((tm, d), lambda i, f, te, na: (i, 0)),
                      pl.BlockSpec((None, d, tf), lambda i, f, te, na: (te[i], 0, f)),
                      pl.BlockSpec((None, d, tf), lambda i, f, te, na: (te[i], 0, f)),
                      pl.BlockSpec((None, tf, d), lambda i, f, te, na: (te[i], f, 0))],
            out_specs=pl.BlockSpec((tm, d), lambda i, f, te, na: (i, 0)),
            scratch_shapes=[pltpu.VMEM((tm, d), BF16), pltpu.VMEM((tm, d), F32)]),
        out_shape=jax.ShapeDtypeStruct((nr, d), F32),
        compiler_params=_params("arbitrary", "arbitrary"),
        name="experts",
    )(tile_expert, n_active, xs, wg, wu, wd)


def _combine_kernel(pos_ref, h_ref, wts_ref, gfin_ref, ys_hbm, o_ref, buf_ref, sem, *, final):
    i = pl.program_id(0)
    tm = h_ref.shape[0]

    def issue(g, carry):
        for u in range(ISSUE_UNROLL):
            r = g * ISSUE_UNROLL + u
            _row_copy(ys_hbm, pos_ref[i, r], buf_ref, r, sem).start()
            _row_copy(ys_hbm, pos_ref[i, tm + r], buf_ref, tm + r, sem).start()
        return carry

    lax.fori_loop(0, tm // ISSUE_UNROLL, issue, 0)
    pltpu.make_async_copy(ys_hbm.at[pl.ds(0, 2 * tm)], buf_ref, sem).wait()
    wts = wts_ref[...]
    out = h_ref[...] + wts[:, 0:1] * buf_ref[0:tm, :] + wts[:, 1:2] * buf_ref[tm:2 * tm, :]
    if final:
        out = _rms(out, gfin_ref[...])
    o_ref[...] = out


def _combine(pos, h, wts, gfin, ys, final):
    t, d = h.shape
    nt = pos.shape[0]
    tm = t // nt
    row = lambda i, p: (i, 0)
    return pl.pallas_call(
        functools.partial(_combine_kernel, final=final),
        grid_spec=pltpu.PrefetchScalarGridSpec(
            num_scalar_prefetch=1, grid=(nt,),
            in_specs=[pl.BlockSpec((tm, d), row), pl.BlockSpec((tm, LANE), row),
                      pl.BlockSpec((1, d), lambda i, p: (0, 0)), pl.BlockSpec(memory_space=pl.ANY)],
            out_specs=pl.BlockSpec((tm, d), row),
            scratch_shapes=[pltpu.VMEM((2 * tm, d), F32), pltpu.SemaphoreType.DMA(())]),
        out_shape=jax.ShapeDtypeStruct((t, d), F32),
        compiler_params=pltpu.CompilerParams(dimension_semantics=("arbitrary",), vmem_limit_bytes=VMEM_LIMIT,
                                             disable_bounds_checks=True),
        name="combine",
    )(pos, h, wts, gfin, ys)


def _moe(h, g, router_w, wg, wu, wd, gfin, final):
    t, d = h.shape
    tm = MOE_TILE
    wr = jnp.pad(router_w, ((0, 0), (0, LANE - N_EXPERTS)))
    hn, idx, wts, cnt = _router(h, g, wr)
    counts = cnt[0, :N_EXPERTS].astype(jnp.int32)
    padded = ((counts + tm - 1) // tm) * tm
    ends = jnp.cumsum(padded)
    offs = ends - padded
    n_tiles = (2 * t + N_EXPERTS * (tm - 1)) // tm
    pos1 = jnp.take(offs, idx[:, 0]) + idx[:, 2]
    pos2 = jnp.take(offs, idx[:, 1]) + idx[:, 3]
    pos = jnp.concatenate([pos1.reshape(-1, ROW_TILE), pos2.reshape(-1, ROW_TILE)], axis=1)
    tile_start = jnp.arange(n_tiles, dtype=jnp.int32) * tm
    tile_expert = jnp.minimum(jnp.sum(tile_start[:, None] >= ends[None, :], axis=1), N_EXPERTS - 1).astype(jnp.int32)
    n_active = (ends[-1:] // tm).astype(jnp.int32)
    xs = _dispatch(pos, hn, jnp.zeros((n_tiles * tm, d), F32))
    ys = _experts(tile_expert, n_active, xs, wg, wu, wd)
    return _combine(pos, h, wts, gfin, ys, final)


def _block_diag(w):
    n, a, b = w.shape
    eye = jnp.eye(n, dtype=w.dtype)
    return jnp.einsum('nij,nm->nimj', w, eye).reshape(n * a, n * b)


def kernel(x, meta_tokens, g_mix, w_in, b_forget, b_gate, conv_w, conv_b, lru_wa, lru_ba, lru_wx, lru_bx, lru_lambda, w_fox_o, w_sb_o, w_lru_o, w_out, g_ffn, ffn_w_gate, ffn_w_up, ffn_w_down, router_w, moe_w_gate, moe_w_up, moe_w_down, g_final):
    batch, seq, d = x.shape
    depth = g_mix.shape[0]
    l = seq + N_META
    lp = -(-l // SEQ_TILE) * SEQ_TILE
    assert (batch * lp) % ROW_TILE == 0 and lp % TIME_TILE == 0 and SEQ_TILE % KV_CHUNK == 0
    meta = jnp.broadcast_to(meta_tokens[None].astype(x.dtype), (batch, N_META, d))
    h = jnp.concatenate([meta, x, jnp.zeros((batch, lp - l, d), x.dtype)], axis=1).reshape(batch * lp, d)

    gfin = g_final.reshape(1, d)
    for layer in range(depth):
        bf = jnp.pad(b_forget[layer], (0, LANE - FOX_HEADS)).reshape(1, LANE)

        qkv, lru, f = _inproj(h, g_mix[layer].reshape(1, d), w_in, layer)
        yl, qaug, kaug = _lru(lru, f, conv_w[layer], conv_b[layer].reshape(1, -1),
                              _block_diag(lru_wa[layer]).astype(BF16), lru_ba[layer].reshape(1, -1),
                              _block_diag(lru_wx[layer]).astype(BF16), lru_bx[layer].reshape(1, -1),
                              lru_lambda[layer].reshape(1, -1), bf, batch)
        yf = _fox(qkv, qaug, kaug, batch)
        ys = _sb(qkv, batch)
        h = _merge(h, g_mix[layer].reshape(1, d), w_in, layer, b_gate[layer].reshape(1, -1), yf, ys, yl,
                   w_fox_o[layer].astype(BF16), w_sb_o[layer].astype(BF16), w_lru_o[layer].astype(BF16),
                   w_out[layer].astype(BF16))

        final = layer == depth - 1
        j = layer // 2
        gf = g_ffn[layer].reshape(1, d)
        if layer % 2 == 0:
            h = _ffn(h, gf, ffn_w_gate[j].astype(BF16), ffn_w_up[j].astype(BF16), ffn_w_down[j].astype(BF16),
                     gfin, final)
        else:
            h = _moe(h, gf, router_w[j], moe_w_gate[j].astype(BF16), moe_w_up[j].astype(BF16),
                     moe_w_down[j].astype(BF16), gfin, final)
    return h.reshape(batch, lp, d)[:, N_META:l]
```

```python
import functools

import jax
import jax.numpy as jnp
from jax import lax
from jax.experimental import pallas as pl
from jax.experimental.pallas import tpu as pltpu

N_META = 16
HEAD_DIM = 128
FOX_HEADS = 4
SB_HEADS = 4
FOX_WIDTH = FOX_HEADS * HEAD_DIM
SB_WIDTH = SB_HEADS * HEAD_DIM
LRU_WIDTH = 512
LRU_BLOCKS = 8
LRU_C = 8.0
CONV_WIDTH = 4
N_BRANCH = 3
N_EXPERTS = 8
RMS_EPS = 1e-6
NEG = -1e30
SOFTPLUS_LINEAR = 40.0
TINY = 1e-30
EXP_ZERO = 105.0
COL_F = 3 * FOX_WIDTH
COL_SB = COL_F + FOX_HEADS
COL_LRU = COL_SB + 3 * SB_WIDTH
COL_GATE = COL_LRU + 2 * LRU_WIDTH

LANE = 128
SUBLANE = 8
SEQ_TILE = 768
KV_CHUNK = 256
ROW_TILE = 512
TIME_TILE = 256
MOE_TILE = 512
MOE_FF_TILE = 1792
ISSUE_UNROLL = 8
VMEM_LIMIT = 56 * 1024 * 1024
GATE_SHIFT = COL_GATE % LANE

F32 = jnp.float32
BF16 = jnp.bfloat16


def _params(*sem):
    return pltpu.CompilerParams(dimension_semantics=sem, vmem_limit_bytes=VMEM_LIMIT)


def _resident(shape):
    return pl.BlockSpec(shape, lambda *_: (0,) * len(shape), pipeline_mode=pl.Buffered(1))


def _rms(x, g):
    return x * lax.rsqrt(jnp.mean(x * x, axis=-1, keepdims=True) + RMS_EPS) * g


def _softplus(x):
    return jnp.maximum(x, 0.0) + jnp.log1p(jnp.exp(-jnp.abs(x)))


def _softplus_scores(x):
    return jnp.where(x > SOFTPLUS_LINEAR, x, jnp.log(1.0 + jnp.exp(x)))


def _inproj_kernel(h_ref, g_ref, w32_ref, qkv_ref, lru_ref, f_ref, wqkv_ref, wlru_ref, wf_ref):
    @pl.when(pl.program_id(0) == 0)
    def _():
        wqkv_ref[:, :COL_F] = w32_ref[:, :COL_F].astype(BF16)
        for c in range(3):
            src = slice(COL_SB + c * SB_WIDTH, COL_SB + (c + 1) * SB_WIDTH)
            wqkv_ref[:, COL_F + c * SB_WIDTH:COL_F + (c + 1) * SB_WIDTH] = w32_ref[:, src].astype(BF16)
        wlru_ref[...] = w32_ref[:, COL_LRU:COL_GATE].astype(BF16)
        wf_ref[...] = w32_ref[:, COL_F:COL_F + LANE].astype(BF16)

    xn = _rms(h_ref[...], g_ref[...]).astype(BF16)
    scale = HEAD_DIM ** -0.5
    for c in range(6):
        cols = slice(c * FOX_WIDTH, (c + 1) * FOX_WIDTH)
        r = jnp.dot(xn, wqkv_ref[:, cols], preferred_element_type=F32)
        if c % 3 == 0:
            r = r * scale
        qkv_ref[:, cols] = r.astype(BF16)
    lru_ref[...] = jnp.dot(xn, wlru_ref[...], preferred_element_type=F32)
    f_ref[...] = jnp.dot(xn, wf_ref[...], preferred_element_type=F32)


def _inproj(h, g, w_in, layer):
    t, d = h.shape
    tm = ROW_TILE
    nqkv = 3 * (FOX_WIDTH + SB_WIDTH)
    nlru = 2 * LRU_WIDTH
    wblock = -(-COL_GATE // LANE) * LANE
    return pl.pallas_call(
        _inproj_kernel,
        grid=(t // tm,),
        in_specs=[pl.BlockSpec((tm, d), lambda i: (i, 0)), _resident((1, d)),
                  pl.BlockSpec((None, d, wblock), lambda i: (layer, 0, 0), pipeline_mode=pl.Buffered(1))],
        out_specs=[pl.BlockSpec((tm, nqkv), lambda i: (i, 0)), pl.BlockSpec((tm, nlru), lambda i: (i, 0)),
                   pl.BlockSpec((tm, LANE), lambda i: (i, 0))],
        out_shape=[jax.ShapeDtypeStruct((t, nqkv), BF16), jax.ShapeDtypeStruct((t, nlru), F32),
                   jax.ShapeDtypeStruct((t, LANE), F32)],
        scratch_shapes=[pltpu.VMEM((d, nqkv), BF16), pltpu.VMEM((d, nlru), BF16), pltpu.VMEM((d, LANE), BF16)],
        compiler_params=_params("arbitrary"),
        name="inproj",
    )(h, g, w_in)


def _shift_rows(a, d, fill):
    rows = lax.broadcasted_iota(jnp.int32, a.shape, 0)
    return jnp.where(rows >= d, pltpu.roll(a, d, 0), fill)


def _lru_kernel(lru_ref, f_ref, cw_ref, cb_ref, wa_ref, ba_ref, wx_ref, bx_ref, lam_ref, bf_ref,
                y_ref, qa_ref, ka_ref, xx_ref, hprev_ref, cprev_ref):
    tt = lru_ref.shape[0]
    w = LRU_WIDTH

    @pl.when(pl.program_id(1) == 0)
    def _():
        xx_ref[...] = jnp.zeros_like(xx_ref)
        hprev_ref[...] = jnp.zeros_like(hprev_ref)
        cprev_ref[...] = jnp.zeros_like(cprev_ref)

    x = lru_ref[:, :w]
    row8 = lax.broadcasted_iota(jnp.int32, (SUBLANE, w), 0)
    shifts = range(1, CONV_WIDTH)
    before = [pltpu.roll(xx_ref[...], s, 0) for s in shifts]
    groups = []
    for g in range(tt // SUBLANE):
        xg = x[g * SUBLANE:(g + 1) * SUBLANE]
        rolled = [pltpu.roll(xg, s, 0) for s in shifts]
        ug = cb_ref[...] + cw_ref[CONV_WIDTH - 1:CONV_WIDTH, :] * xg
        for s in shifts:
            tap = cw_ref[CONV_WIDTH - 1 - s:CONV_WIDTH - s, :]
            ug = ug + tap * jnp.where(row8 >= s, rolled[s - 1], before[s - 1])
        groups.append(ug)
        before = rolled
    u = jnp.concatenate(groups, axis=0)
    xx_ref[...] = x[tt - SUBLANE:, :]

    ub = u.astype(BF16)
    r = jax.nn.sigmoid(jnp.dot(ub, wa_ref[...], preferred_element_type=F32) + ba_ref[...])
    gi = jax.nn.sigmoid(jnp.dot(ub, wx_ref[...], preferred_element_type=F32) + bx_ref[...])
    log_a = (-LRU_C) * r * _softplus(-lam_ref[...])
    a = jnp.exp(log_a)
    gap = 1.0 - a * a
    b = (gap * lax.rsqrt(jnp.maximum(gap, TINY))) * (gi * u)

    state = hprev_ref[...]
    groups = []
    for g in range(tt // SUBLANE):
        ag = a[g * SUBLANE:(g + 1) * SUBLANE]
        bg = b[g * SUBLANE:(g + 1) * SUBLANE]
        d = 1
        while d < SUBLANE:
            keep = row8 >= d
            bg = ag * jnp.where(keep, pltpu.roll(bg, d, 0), 0.0) + bg
            ag = ag * jnp.where(keep, pltpu.roll(ag, d, 0), 1.0)
            d *= 2
        hg = bg + ag * state
        groups.append(hg)
        state = hg[SUBLANE - 1:, :]
    h = jnp.concatenate(groups, axis=0)
    hprev_ref[...] = state
    y_ref[...] = (h * jax.nn.gelu(lru_ref[:, w:])).astype(y_ref.dtype)

    c = -_softplus(-(f_ref[...] + bf_ref[...]))
    d = 1
    while d < tt:
        c = c + _shift_rows(c, d, 0.0)
        d *= 2
    c = c + cprev_ref[...]
    cprev_ref[...] = c[tt - 1:tt, :]
    lane = lax.broadcasted_iota(jnp.int32, (tt, LANE), 1)
    term = jnp.where(lane >= 3, lane - 3, lane)
    for hd in range(FOX_HEADS):
        ch = jnp.broadcast_to(c[:, hd:hd + 1], (tt, LANE))
        hi = ch.astype(BF16).astype(F32)
        mid = (ch - hi).astype(BF16).astype(F32)
        lo = ((ch - hi) - mid).astype(BF16).astype(F32)
        c3 = jnp.where(term == 0, hi, jnp.where(term == 1, mid, lo))
        cols = slice(hd * LANE, (hd + 1) * LANE)
        qa_ref[:, cols] = jnp.where(lane < 3, c3, jnp.where(lane < 6, 1.0, 0.0)).astype(BF16)
        ka_ref[:, cols] = jnp.where(lane < 3, 1.0, jnp.where(lane < 6, -c3, 0.0)).astype(BF16)


def _lru(lru, f, cw, cb, wa, ba, wx, bx, lam, bf, batch):
    t = lru.shape[0]
    lp = t // batch
    tt = TIME_TILE
    nt = lp // tt
    w = LRU_WIDTH
    row = lambda b, i: (b * nt + i, 0)
    return pl.pallas_call(
        _lru_kernel,
        grid=(batch, nt),
        in_specs=[pl.BlockSpec((tt, 2 * w), row), pl.BlockSpec((tt, LANE), row),
                  _resident(cw.shape), _resident((1, w)), _resident((w, w)), _resident((1, w)),
                  _resident((w, w)), _resident((1, w)), _resident((1, w)), _resident((1, LANE))],
        out_specs=[pl.BlockSpec((tt, w), row), pl.BlockSpec((tt, FOX_WIDTH), row),
                   pl.BlockSpec((tt, FOX_WIDTH), row)],
        out_shape=[jax.ShapeDtypeStruct((t, w), BF16), jax.ShapeDtypeStruct((t, FOX_WIDTH), BF16),
                   jax.ShapeDtypeStruct((t, FOX_WIDTH), BF16)],
        scratch_shapes=[pltpu.VMEM((SUBLANE, w), F32), pltpu.VMEM((1, w), F32), pltpu.VMEM((1, LANE), F32)],
        compiler_params=_params("parallel", "arbitrary"),
        name="lru",
    )(lru, f, cw, cb, wa, ba, wx, bx, lam, bf)


def _wide(a, width):
    return jnp.concatenate([a] * (width // LANE), axis=1)


def _row_norm_max(x):
    x = x.astype(F32)
    return jnp.sqrt(jnp.max(jnp.sum(x * x, axis=1, keepdims=True)))


def _fox_kernel(q_ref, qa_ref, k_ref, ka_ref, v_ref, o_ref, m_ref, acc_ref, kn_ref):
    i = pl.program_id(1)
    tq = q_ref.shape[0]
    tk = KV_CHUNK
    nd = tq // tk
    hd = HEAD_DIM
    ones = jnp.ones((tk, hd), BF16)

    @pl.when(i == 0)
    def _():
        for h in range(FOX_HEADS):
            def blk(r, best, h=h):
                rows = pl.ds(pl.multiple_of(r * tq, tq), tq)
                return jnp.maximum(best, _row_norm_max(k_ref[rows, h * hd:(h + 1) * hd]))
            kn_ref[h] = lax.fori_loop(0, k_ref.shape[0] // tq, blk, jnp.float32(0.0))

    m_ref[...] = jnp.full_like(m_ref, NEG)
    acc_ref[...] = jnp.zeros_like(acc_ref)

    def step(j, r0):
        off = pl.multiple_of(j * tk, tk)
        rows = slice(r0 or 0, tq)
        for h in range(FOX_HEADS):
            cols = slice(h * hd, (h + 1) * hd)
            q2 = jnp.concatenate([q_ref[rows, cols], qa_ref[rows, cols]], axis=1)
            k2 = jnp.concatenate([k_ref[pl.ds(off, tk), cols], ka_ref[pl.ds(off, tk), cols]], axis=1)
            v2 = jnp.concatenate([v_ref[pl.ds(off, tk), cols], ones], axis=1)
            s = lax.dot_general(q2, k2, (((1,), (1,)), ((), ())), preferred_element_type=F32)
            if r0 is not None:
                qpos = i * tq + r0 + lax.broadcasted_iota(jnp.int32, (tq - r0, 1), 0)
                kpos = j * tk + lax.broadcasted_iota(jnp.int32, (1, tk), 1)
                s = jnp.where(kpos <= qpos, s, NEG)
            m_prev = m_ref[h, rows, :]
            m_new = jnp.maximum(m_prev, jnp.max(s, axis=-1, keepdims=True))
            alpha = jnp.exp(m_prev - m_new)
            p = jnp.exp(s - _wide(m_new, tk))
            acc_ref[h, rows, :] = (_wide(alpha, 2 * hd) * acc_ref[h, rows, :]
                                   + jnp.dot(p.astype(BF16), v2, preferred_element_type=F32))
            m_ref[h, rows, :] = m_new

    for dd in range(nd):
        step(i * nd + dd, dd * tk)

    qn = [_row_norm_max(q_ref[:, h * hd:(h + 1) * hd]) for h in range(FOX_HEADS)]

    def negligible(g):
        last = pl.ds(pl.multiple_of((g + 1) * tq - 2 * SUBLANE, 2 * SUBLANE), 2 * SUBLANE)
        done = None
        for h in range(FOX_HEADS):
            cols = slice(h * hd, (h + 1) * hd)
            k_last = ka_ref[last, cols][2 * SUBLANE - 1:, :].astype(F32)
            bias = jnp.sum(qa_ref[0:1, cols].astype(F32) * k_last)
            small = qn[h] * kn_ref[h] + bias < jnp.min(m_ref[h]) - EXP_ZERO
            done = small if done is None else jnp.logical_and(done, small)
        return done

    def more(state):
        g, done = state
        return jnp.logical_and(g >= 0, jnp.logical_not(done))

    def body(state):
        g, _ = state
        for dd in range(nd - 1, -1, -1):
            step(g * nd + dd, None)
        return g - 1, negligible(jnp.maximum(g - 1, 0))

    lax.while_loop(more, body, (i - 1, negligible(jnp.maximum(i - 1, 0))))
    for h in range(FOX_HEADS):
        o_ref[:, h * hd:(h + 1) * hd] = (acc_ref[h, :, :hd] / acc_ref[h, :, hd:]).astype(o_ref.dtype)


def _seq_resident(lp, width, col):
    return pl.BlockSpec((lp, width), lambda b, i: (b, col), pipeline_mode=pl.Buffered(1))


def _fox(qkv, qaug, kaug, batch):
    t = qkv.shape[0]
    lp = t // batch
    tq = SEQ_TILE
    nq = lp // tq
    w = FOX_WIDTH
    qmap = lambda b, i: (b * nq + i, 0)
    return pl.pallas_call(
        _fox_kernel,
        grid=(batch, nq),
        in_specs=[pl.BlockSpec((tq, w), qmap), pl.BlockSpec((tq, w), qmap),
                  _seq_resident(lp, w, 1), _seq_resident(lp, w, 0), _seq_resident(lp, w, 2)],
        out_specs=pl.BlockSpec((tq, w), qmap),
        out_shape=jax.ShapeDtypeStruct((t, w), BF16),
        scratch_shapes=[pltpu.VMEM((FOX_HEADS, tq, LANE), F32), pltpu.VMEM((FOX_HEADS, tq, 2 * HEAD_DIM), F32),
                        pltpu.SMEM((FOX_HEADS,), F32)],
        compiler_params=_params("arbitrary", "arbitrary"),
        name="fox",
    )(qkv, qaug, qkv, kaug, qkv)


def _sb_kernel(q_ref, k_ref, v_ref, o_ref, carry_ref, acc_ref):
    i = pl.program_id(1)
    tq = q_ref.shape[0]
    tk = KV_CHUNK
    nd = tq // tk
    hd = HEAD_DIM
    rr = lax.broadcasted_iota(jnp.int32, (tk, tk), 0)
    cc = lax.broadcasted_iota(jnp.int32, (tk, tk), 1)
    later_mat = jnp.where(rr > cc, 1.0, 0.0).astype(BF16)
    carry_ref[...] = jnp.zeros_like(carry_ref)
    acc_ref[...] = jnp.zeros_like(acc_ref)

    def step(j, r0):
        off = pl.multiple_of(j * tk, tk)
        rows = slice(r0 or 0, tq)
        for h in range(SB_HEADS):
            cols = slice(h * hd, (h + 1) * hd)
            z = lax.dot_general(q_ref[rows, cols], k_ref[pl.ds(off, tk), cols], (((1,), (1,)), ((), ())),
                                preferred_element_type=F32)
            sp = _softplus_scores(z)
            spm = sp
            if r0 is not None:
                qpos = i * tq + r0 + lax.broadcasted_iota(jnp.int32, (tq - r0, 1), 0)
                mask = (j * tk + lax.broadcasted_iota(jnp.int32, (1, tk), 1)) < qpos
                spm = jnp.where(mask, sp, 0.0)
            cum = (jnp.dot(spm.astype(BF16), later_mat, preferred_element_type=F32)
                   + _wide(carry_ref[h, rows, :], tk))
            a = jnp.exp((z - sp) - cum)
            if r0 is not None:
                a = jnp.where(mask, a, 0.0)
            acc_ref[h, rows, :] += jnp.dot(a.astype(BF16), v_ref[pl.ds(off, tk), cols], preferred_element_type=F32)
            carry_ref[h, rows, :] = jnp.broadcast_to(cum[:, 0:1] + spm[:, 0:1], (tq - rows.start, LANE))

    for dd in range(nd - 1, -1, -1):
        step(i * nd + dd, dd * tk)

    def least_carry():
        least = carry_ref[0]
        for h in range(1, SB_HEADS):
            least = jnp.minimum(least, carry_ref[h])
        return jnp.min(least)

    def more(state):
        j, least = state
        return jnp.logical_and(j >= 0, least < EXP_ZERO)

    def body(state):
        j, _ = state
        step(j, None)
        return j - 1, least_carry()

    lax.while_loop(more, body, (i * nd - 1, least_carry()))
    for h in range(SB_HEADS):
        o_ref[:, h * hd:(h + 1) * hd] = acc_ref[h].astype(o_ref.dtype)


def _sb(qkv, batch):
    t = qkv.shape[0]
    lp = t // batch
    tq = SEQ_TILE
    nq = lp // tq
    w = SB_WIDTH
    base = 3 * FOX_WIDTH // w
    return pl.pallas_call(
        _sb_kernel,
        grid=(batch, nq),
        in_specs=[pl.BlockSpec((tq, w), lambda b, i: (b * nq + i, base)),
                  _seq_resident(lp, w, base + 1), _seq_resident(lp, w, base + 2)],
        out_specs=pl.BlockSpec((tq, w), lambda b, i: (b * nq + i, 0)),
        out_shape=jax.ShapeDtypeStruct((t, w), BF16),
        scratch_shapes=[pltpu.VMEM((SB_HEADS, tq, LANE), F32), pltpu.VMEM((SB_HEADS, tq, HEAD_DIM), F32)],
        compiler_params=_params("parallel", "arbitrary"),
        name="stickbreak",
    )(qkv, qkv, qkv)


def _merge_kernel(h_ref, g_ref, wg0_ref, wg1_ref, wg2_ref, wgt_ref, bgate_ref, yf_ref, ys_ref, yl_ref,
                  wfo_ref, wso_ref, wlo_ref, wout_ref, o_ref, wgate_ref):
    @pl.when(pl.program_id(0) == 0)
    def _():
        blocks = (wg0_ref, wg1_ref, wg2_ref, wgt_ref)
        dd = h_ref.shape[1]
        for br in range(N_BRANCH):
            both = jnp.concatenate([blocks[br][...], blocks[br + 1][:, :LANE]], axis=1)
            wgate_ref[:, br * dd:(br + 1) * dd] = both[:, GATE_SHIFT:GATE_SHIFT + dd].astype(BF16)

    h = h_ref[...]
    d = h.shape[1]
    xn = _rms(h, g_ref[...]).astype(BF16)
    merged = None
    for br, (y_ref, w_ref) in enumerate(((yf_ref, wfo_ref), (ys_ref, wso_ref), (yl_ref, wlo_ref))):
        cols = slice(br * d, (br + 1) * d)
        gate = jax.nn.sigmoid(jnp.dot(xn, wgate_ref[:, cols], preferred_element_type=F32) + bgate_ref[:, cols])
        term = gate * jnp.dot(y_ref[...], w_ref[...], preferred_element_type=F32)
        merged = term if merged is None else merged + term
    o_ref[...] = h + jnp.dot(merged.astype(BF16), wout_ref[...], preferred_element_type=F32)


def _merge(h, g, w_in, layer, bgate, yf, ys, yl, wfo, wso, wlo, wout):
    t, d = h.shape
    tm = ROW_TILE
    row = lambda i: (i, 0)
    first = COL_GATE - GATE_SHIFT
    assert first % d == 0
    wspec = lambda width, blk: pl.BlockSpec((None, d, width), lambda i: (layer, 0, blk),
                                            pipeline_mode=pl.Buffered(1))
    return pl.pallas_call(
        _merge_kernel,
        grid=(t // tm,),
        in_specs=[pl.BlockSpec((tm, d), row), _resident((1, d)),
                  wspec(d, first // d), wspec(d, first // d + 1), wspec(d, first // d + 2),
                  wspec(LANE, (first + N_BRANCH * d) // LANE), _resident(bgate.shape),
                  pl.BlockSpec((tm, yf.shape[1]), row), pl.BlockSpec((tm, ys.shape[1]), row),
                  pl.BlockSpec((tm, yl.shape[1]), row),
                  _resident(wfo.shape), _resident(wso.shape), _resident(wlo.shape), _resident(wout.shape)],
        out_specs=pl.BlockSpec((tm, d), row),
        out_shape=jax.ShapeDtypeStruct((t, d), F32),
        scratch_shapes=[pltpu.VMEM((d, N_BRANCH * d), BF16)],
        compiler_params=_params("arbitrary"),
        name="merge",
    )(h, g, w_in, w_in, w_in, w_in, bgate, yf, ys, yl, wfo, wso, wlo, wout)


def _ffn_chunk(ff):
    best = LANE
    for c in range(LANE, 1536 + 1, LANE):
        if ff % c == 0:
            best = c
    return best


def _ffn_kernel(h_ref, g_ref, wg_ref, wu_ref, wd_ref, gfin_ref, o_ref, *, final):
    h = h_ref[...]
    hn = _rms(h, g_ref[...]).astype(BF16)
    ff = wg_ref.shape[1]
    fc = _ffn_chunk(ff)
    out = h
    for c in range(ff // fc):
        cols = slice(c * fc, (c + 1) * fc)
        act = jax.nn.silu(jnp.dot(hn, wg_ref[:, cols], preferred_element_type=F32))
        act = act * jnp.dot(hn, wu_ref[:, cols], preferred_element_type=F32)
        out = out + jnp.dot(act.astype(BF16), wd_ref[cols, :], preferred_element_type=F32)
    if final:
        out = _rms(out, gfin_ref[...])
    o_ref[...] = out


def _ffn(h, g, wg, wu, wd, gfin, final):
    t, d = h.shape
    tm = ROW_TILE
    row = lambda i: (i, 0)
    return pl.pallas_call(
        functools.partial(_ffn_kernel, final=final),
        grid=(t // tm,),
        in_specs=[pl.BlockSpec((tm, d), row), _resident((1, d)), _resident(wg.shape), _resident(wu.shape),
                  _resident(wd.shape), _resident((1, d))],
        out_specs=pl.BlockSpec((tm, d), row),
        out_shape=jax.ShapeDtypeStruct((t, d), F32),
        compiler_params=_params("parallel"),
        name="ffn",
    )(h, g, wg, wu, wd, gfin)


def _router_kernel(h_ref, g_ref, wr_ref, hn_ref, idx_ref, wts_ref, cnt_ref, run_ref):
    tm = h_ref.shape[0]

    @pl.when(pl.program_id(0) == 0)
    def _():
        run_ref[...] = jnp.zeros_like(run_ref)

    hn = _rms(h_ref[...], g_ref[...])
    hn_ref[...] = hn
    logits = jnp.dot(hn, wr_ref[...], preferred_element_type=F32, precision=lax.Precision.HIGHEST)
    lane = lax.broadcasted_iota(jnp.int32, (tm, LANE), 1)
    lg = jnp.where(lane < N_EXPERTS, logits, -jnp.inf)
    v1 = jnp.max(lg, axis=-1, keepdims=True)
    i1 = jnp.min(jnp.where(lg == v1, lane, LANE), axis=-1, keepdims=True)
    lg2 = jnp.where(lane == i1, -jnp.inf, lg)
    v2 = jnp.max(lg2, axis=-1, keepdims=True)
    i2 = jnp.min(jnp.where(lg2 == v2, lane, LANE), axis=-1, keepdims=True)
    e = jnp.exp(v2 - v1)
    w1 = 1.0 / (1.0 + e)
    w2 = e * w1
    sel1 = lane == i1
    sel2 = lane == i2
    onehot = jnp.where(sel1, 1.0, 0.0) + jnp.where(sel2, 1.0, 0.0)
    rr = lax.broadcasted_iota(jnp.int32, (tm, tm), 0)
    cc = lax.broadcasted_iota(jnp.int32, (tm, tm), 1)
    before = jnp.where(cc < rr, 1.0, 0.0).astype(BF16)
    excl = jnp.dot(before, onehot.astype(BF16), preferred_element_type=F32) + run_ref[...]
    r1 = jnp.sum(jnp.where(sel1, excl, 0.0), axis=-1, keepdims=True)
    r2 = jnp.sum(jnp.where(sel2, excl, 0.0), axis=-1, keepdims=True)
    run = run_ref[...] + jnp.sum(onehot, axis=0, keepdims=True)
    run_ref[...] = run
    idx = jnp.where(lane == 0, i1, jnp.where(lane == 1, i2, 0))
    rank = jnp.where(lane == 2, r1, jnp.where(lane == 3, r2, 0.0)).astype(jnp.int32)
    idx_ref[...] = idx + rank
    wts_ref[...] = jnp.where(lane == 0, w1, jnp.where(lane == 1, w2, 0.0))
    cnt_ref[...] = jnp.broadcast_to(run, cnt_ref.shape)


def _router(h, g, wr):
    t, d = h.shape
    tm = ROW_TILE
    row = lambda i: (i, 0)
    return pl.pallas_call(
        _router_kernel,
        grid=(t // tm,),
        in_specs=[pl.BlockSpec((tm, d), row), _resident((1, d)), _resident(wr.shape)],
        out_specs=[pl.BlockSpec((tm, d), row), pl.BlockSpec((tm, LANE), row), pl.BlockSpec((tm, LANE), row),
                   pl.BlockSpec((SUBLANE, LANE), lambda i: (0, 0))],
        out_shape=[jax.ShapeDtypeStruct((t, d), F32), jax.ShapeDtypeStruct((t, LANE), jnp.int32),
                   jax.ShapeDtypeStruct((t, LANE), F32), jax.ShapeDtypeStruct((SUBLANE, LANE), F32)],
        scratch_shapes=[pltpu.VMEM((1, LANE), F32)],
        compiler_params=_params("arbitrary"),
        name="router",
    )(h, g, wr)


def _row_copy(src, s, dst, d, sem):
    return pltpu.make_async_copy(src.at[pl.ds(s, 1)], dst.at[pl.ds(d, 1)], sem)


def _dispatch_kernel(pos_ref, tails_ref, hn_ref, xs_hbm, zero_ref, sem):
    i = pl.program_id(0)
    tm = hn_ref.shape[0]

    @pl.when(i == 0)
    def _():
        zero_ref[...] = jnp.zeros_like(zero_ref)
        tz = zero_ref.shape[0]
        tails = [pl.ds(pl.multiple_of(tails_ref[e], SUBLANE), tz) for e in range(N_EXPERTS)]
        tails += [pl.ds(xs_hbm.shape[0] - (e + 1) * tz, tz) for e in range(N_EXPERTS)]
        for rows in tails:
            clear = pltpu.make_async_copy(zero_ref, xs_hbm.at[rows], sem)
            clear.start()
            clear.wait()

    def issue(g, carry):
        for u in range(ISSUE_UNROLL):
            r = g * ISSUE_UNROLL + u
            _row_copy(hn_ref, r, xs_hbm, pos_ref[i, r], sem).start()
            _row_copy(hn_ref, r, xs_hbm, pos_ref[i, tm + r], sem).start()
        return carry

    lax.fori_loop(0, tm // ISSUE_UNROLL, issue, 0)
    for _ in range(2):
        pltpu.make_async_copy(hn_ref, xs_hbm.at[pl.ds(0, tm)], sem).wait()


def _dispatch(pos, tails, hn, n_rows):
    nt = pos.shape[0]
    t, d = hn.shape
    return pl.pallas_call(
        _dispatch_kernel,
        grid_spec=pltpu.PrefetchScalarGridSpec(
            num_scalar_prefetch=2, grid=(nt,),
            in_specs=[pl.BlockSpec((t // nt, d), lambda i, p, tl: (i, 0))],
            out_specs=pl.BlockSpec(memory_space=pl.ANY),
            scratch_shapes=[pltpu.VMEM((MOE_TILE, d), F32), pltpu.SemaphoreType.DMA(())]),
        out_shape=jax.ShapeDtypeStruct((n_rows, d), F32),
        compiler_params=pltpu.CompilerParams(dimension_semantics=("arbitrary",), has_side_effects=True,
                                             vmem_limit_bytes=VMEM_LIMIT, disable_bounds_checks=True),
        name="dispatch",
    )(pos, tails, hn)


def _experts_kernel(te_ref, na_ref, xs_ref, wg_ref, wu_ref, wd_ref, ys_ref, xb_ref, acc_ref):
    del te_ref
    i = pl.program_id(0)
    f = pl.program_id(1)

    @pl.when(f == 0)
    def _():
        xb_ref[...] = xs_ref[...].astype(BF16)
        acc_ref[...] = jnp.zeros_like(acc_ref)

    @pl.when(i < na_ref[0])
    def _():
        xb = xb_ref[...]
        act = jax.nn.silu(jnp.dot(xb, wg_ref[...], preferred_element_type=F32))
        act = act * jnp.dot(xb, wu_ref[...], preferred_element_type=F32)
        acc_ref[...] += jnp.dot(act.astype(BF16), wd_ref[...], preferred_element_type=F32)

    @pl.when(f == pl.num_programs(1) - 1)
    def _():
        ys_ref[...] = acc_ref[...]


def _experts(tile_expert, n_active, xs, wg, wu, wd):
    nr, d = xs.shape
    tm = MOE_TILE
    ff = wg.shape[2]
    tf = MOE_FF_TILE if ff % MOE_FF_TILE == 0 else ff
    return pl.pallas_call(
        _experts_kernel,
        grid_spec=pltpu.PrefetchScalarGridSpec(
            num_scalar_prefetch=2, grid=(nr // tm, ff // tf),
            in_specs=[pl.BlockSpec((tm, d), lambda i, f, te, na: (jnp.minimum(i, na[0] - 1), 0)),
                      pl.BlockSpec((None, d, tf), lambda i, f, te, na: (te[i], 0, f)),
                      pl.BlockSpec((None, d, tf), lambda i, f, te, na: (te[i], 0, f)),
                      pl.BlockSpec((None, tf, d), lambda i, f, te, na: (te[i], f, 0))],
            out_specs=pl.BlockSpec((tm, d), lambda i, f, te, na: (i, 0)),
            scratch_shapes=[pltpu.VMEM((tm, d), BF16), pltpu.VMEM((tm, d), F32)]),
        out_shape=jax.ShapeDtypeStruct((nr, d), F32),
        compiler_params=_params("arbitrary", "arbitrary"),
        name="experts",
    )(tile_expert, n_active, xs, wg, wu, wd)


def _combine_kernel(pos_ref, h_ref, wts_ref, gfin_ref, ys_hbm, o_ref, buf_ref, sem, *, final):
    i = pl.program_id(0)
    tm = h_ref.shape[0]

    def issue(g, carry):
        for u in range(ISSUE_UNROLL):
            r = g * ISSUE_UNROLL + u
            _row_copy(ys_hbm, pos_ref[i, r], buf_ref, r, sem).start()
            _row_copy(ys_hbm, pos_ref[i, tm + r], buf_ref, tm + r, sem).start()
        return carry

    lax.fori_loop(0, tm // ISSUE_UNROLL, issue, 0)
    pltpu.make_async_copy(ys_hbm.at[pl.ds(0, 2 * tm)], buf_ref, sem).wait()
    wts = wts_ref[...]
    out = h_ref[...] + wts[:, 0:1] * buf_ref[0:tm, :] + wts[:, 1:2] * buf_ref[tm:2 * tm, :]
    if final:
        out = _rms(out, gfin_ref[...])
    o_ref[...] = out


def _combine(pos, h, wts, gfin, ys, final):
    t, d = h.shape
    nt = pos.shape[0]
    tm = t // nt
    row = lambda i, p: (i, 0)
    return pl.pallas_call(
        functools.partial(_combine_kernel, final=final),
        grid_spec=pltpu.PrefetchScalarGridSpec(
            num_scalar_prefetch=1, grid=(nt,),
            in_specs=[pl.BlockSpec((tm, d), row), pl.BlockSpec((tm, LANE), row),
                      pl.BlockSpec((1, d), lambda i, p: (0, 0)), pl.BlockSpec(memory_space=pl.ANY)],
            out_specs=pl.BlockSpec((tm, d), row),
            scratch_shapes=[pltpu.VMEM((2 * tm, d), F32), pltpu.SemaphoreType.DMA(())]),
        out_shape=jax.ShapeDtypeStruct((t, d), F32),
        compiler_params=pltpu.CompilerParams(dimension_semantics=("arbitrary",), vmem_limit_bytes=VMEM_LIMIT,
                                             disable_bounds_checks=True),
        name="combine",
    )(pos, h, wts, gfin, ys)


def _moe(h, g, router_w, wg, wu, wd, gfin, final):
    t, d = h.shape
    tm = MOE_TILE
    wr = jnp.pad(router_w, ((0, 0), (0, LANE - N_EXPERTS)))
    hn, idx, wts, cnt = _router(h, g, wr)
    counts = cnt[0, :N_EXPERTS].astype(jnp.int32)
    padded = ((counts + tm - 1) // tm) * tm
    ends = jnp.cumsum(padded)
    offs = ends - padded
    n_tiles = (2 * t + N_EXPERTS * (tm - 1)) // tm
    pos1 = jnp.take(offs, idx[:, 0]) + idx[:, 2]
    pos2 = jnp.take(offs, idx[:, 1]) + idx[:, 3]
    pos = jnp.concatenate([pos1.reshape(-1, ROW_TILE), pos2.reshape(-1, ROW_TILE)], axis=1)
    tile_start = jnp.arange(n_tiles, dtype=jnp.int32) * tm
    tile_expert = jnp.minimum(jnp.sum(tile_start[:, None] >= ends[None, :], axis=1), N_EXPERTS - 1).astype(jnp.int32)
    n_active = (ends[-1:] // tm).astype(jnp.int32)
    tails = jnp.where(padded > 0, ends - tm, 0).astype(jnp.int32)
    xs = _dispatch(pos, tails, hn, n_tiles * tm)
    ys = _experts(tile_expert, n_active, xs, wg, wu, wd)
    return _combine(pos, h, wts, gfin, ys, final)


def _block_diag(w):
    n, a, b = w.shape
    eye = jnp.eye(n, dtype=w.dtype)
    return jnp.einsum('nij,nm->nimj', w, eye).reshape(n * a, n * b)


def kernel(x, meta_tokens, g_mix, w_in, b_forget, b_gate, conv_w, conv_b, lru_wa, lru_ba, lru_wx, lru_bx, lru_lambda, w_fox_o, w_sb_o, w_lru_o, w_out, g_ffn, ffn_w_gate, ffn_w_up, ffn_w_down, router_w, moe_w_gate, moe_w_up, moe_w_down, g_final):
    batch, seq, d = x.shape
    depth = g_mix.shape[0]
    l = seq + N_META
    lp = -(-l // SEQ_TILE) * SEQ_TILE
    assert (batch * lp) % ROW_TILE == 0 and lp % TIME_TILE == 0 and SEQ_TILE % KV_CHUNK == 0
    meta = jnp.broadcast_to(meta_tokens[None].astype(x.dtype), (batch, N_META, d))
    h = jnp.concatenate([meta, x, jnp.zeros((batch, lp - l, d), x.dtype)], axis=1).reshape(batch * lp, d)

    gfin = g_final.reshape(1, d)
    for layer in range(depth):
        bf = jnp.pad(b_forget[layer], (0, LANE - FOX_HEADS)).reshape(1, LANE)

        qkv, lru, f = _inproj(h, g_mix[layer].reshape(1, d), w_in, layer)
        yl, qaug, kaug = _lru(lru, f, conv_w[layer], conv_b[layer].reshape(1, -1),
                              _block_diag(lru_wa[layer]).astype(BF16), lru_ba[layer].reshape(1, -1),
                              _block_diag(lru_wx[layer]).astype(BF16), lru_bx[layer].reshape(1, -1),
                              lru_lambda[layer].reshape(1, -1), bf, batch)
        yf = _fox(qkv, qaug, kaug, batch)
        ys = _sb(qkv, batch)
        h = _merge(h, g_mix[layer].reshape(1, d), w_in, layer, b_gate[layer].reshape(1, -1), yf, ys, yl,
                   w_fox_o[layer].astype(BF16), w_sb_o[layer].astype(BF16), w_lru_o[layer].astype(BF16),
                   w_out[layer].astype(BF16))

        final = layer == depth - 1
        j = layer // 2
        gf = g_ffn[layer].reshape(1, d)
        if layer % 2 == 0:
            h = _ffn(h, gf, ffn_w_gate[j].astype(BF16), ffn_w_up[j].astype(BF16), ffn_w_down[j].astype(BF16),
                     gfin, final)
        else:
            h = _moe(h, gf, router_w[j], moe_w_gate[j].astype(BF16), moe_w_up[j].astype(BF16),
                     moe_w_down[j].astype(BF16), gfin, final)
    return h.reshape(batch, lp, d)[:, N_META:l]
```

```python
import functools

import jax
import jax.numpy as jnp
from jax import lax
from jax.experimental import pallas as pl
from jax.experimental.pallas import tpu as pltpu

N_META = 16
HEAD_DIM = 128
FOX_HEADS = 4
SB_HEADS = 4
FOX_WIDTH = FOX_HEADS * HEAD_DIM
SB_WIDTH = SB_HEADS * HEAD_DIM
LRU_WIDTH = 512
LRU_BLOCKS = 8
LRU_C = 8.0
CONV_WIDTH = 4
N_BRANCH = 3
N_EXPERTS = 8
RMS_EPS = 1e-6
NEG = -1e30
SOFTPLUS_LINEAR = 40.0
TINY = 1e-30
EXP_ZERO = 105.0
EXP_SAFE = 40.0
COL_F = 3 * FOX_WIDTH
COL_SB = COL_F + FOX_HEADS
COL_LRU = COL_SB + 3 * SB_WIDTH
COL_GATE = COL_LRU + 2 * LRU_WIDTH

LANE = 128
SUBLANE = 8
SEQ_TILE = 768
KV_CHUNK = 256
ROW_TILE = 512
TIME_TILE = 256
MOE_TILE = 512
MOE_FF_TILE = 1792
ISSUE_UNROLL = 8
VMEM_LIMIT = 56 * 1024 * 1024
GATE_SHIFT = COL_GATE % LANE

F32 = jnp.float32
BF16 = jnp.bfloat16


def _params(*sem):
    return pltpu.CompilerParams(dimension_semantics=sem, vmem_limit_bytes=VMEM_LIMIT)


def _resident(shape):
    return pl.BlockSpec(shape, lambda *_: (0,) * len(shape), pipeline_mode=pl.Buffered(1))


def _rms(x, g):
    return x * lax.rsqrt(jnp.mean(x * x, axis=-1, keepdims=True) + RMS_EPS) * g


def _softplus(x):
    return jnp.maximum(x, 0.0) + jnp.log1p(jnp.exp(-jnp.abs(x)))


def _softplus_scores(x):
    return jnp.where(x > SOFTPLUS_LINEAR, x, jnp.log(1.0 + jnp.exp(x)))


def _inproj_kernel(h_ref, g_ref, w32_ref, qkv_ref, lru_ref, f_ref, wqkv_ref, wlru_ref, wf_ref):
    @pl.when(pl.program_id(0) == 0)
    def _():
        wqkv_ref[:, :COL_F] = w32_ref[:, :COL_F].astype(BF16)
        for c in range(3):
            src = slice(COL_SB + c * SB_WIDTH, COL_SB + (c + 1) * SB_WIDTH)
            wqkv_ref[:, COL_F + c * SB_WIDTH:COL_F + (c + 1) * SB_WIDTH] = w32_ref[:, src].astype(BF16)
        wlru_ref[...] = w32_ref[:, COL_LRU:COL_GATE].astype(BF16)
        wf_ref[...] = w32_ref[:, COL_F:COL_F + LANE].astype(BF16)

    xn = _rms(h_ref[...], g_ref[...]).astype(BF16)
    scale = HEAD_DIM ** -0.5
    for c in range(6):
        cols = slice(c * FOX_WIDTH, (c + 1) * FOX_WIDTH)
        r = jnp.dot(xn, wqkv_ref[:, cols], preferred_element_type=F32)
        if c % 3 == 0:
            r = r * scale
        qkv_ref[:, cols] = r.astype(BF16)
    lru_ref[...] = jnp.dot(xn, wlru_ref[...], preferred_element_type=F32)
    f_ref[...] = jnp.dot(xn, wf_ref[...], preferred_element_type=F32)


def _inproj(h, g, w_in, layer):
    t, d = h.shape
    tm = ROW_TILE
    nqkv = 3 * (FOX_WIDTH + SB_WIDTH)
    nlru = 2 * LRU_WIDTH
    wblock = -(-COL_GATE // LANE) * LANE
    return pl.pallas_call(
        _inproj_kernel,
        grid=(t // tm,),
        in_specs=[pl.BlockSpec((tm, d), lambda i: (i, 0)), _resident((1, d)),
                  pl.BlockSpec((None, d, wblock), lambda i: (layer, 0, 0), pipeline_mode=pl.Buffered(1))],
        out_specs=[pl.BlockSpec((tm, nqkv), lambda i: (i, 0)), pl.BlockSpec((tm, nlru), lambda i: (i, 0)),
                   pl.BlockSpec((tm, LANE), lambda i: (i, 0))],
        out_shape=[jax.ShapeDtypeStruct((t, nqkv), BF16), jax.ShapeDtypeStruct((t, nlru), F32),
                   jax.ShapeDtypeStruct((t, LANE), F32)],
        scratch_shapes=[pltpu.VMEM((d, nqkv), BF16), pltpu.VMEM((d, nlru), BF16), pltpu.VMEM((d, LANE), BF16)],
        compiler_params=_params("arbitrary"),
        name="inproj",
    )(h, g, w_in)


def _shift_rows(a, d, fill):
    rows = lax.broadcasted_iota(jnp.int32, a.shape, 0)
    return jnp.where(rows >= d, pltpu.roll(a, d, 0), fill)


def _lru_kernel(lru_ref, f_ref, cw_ref, cb_ref, wa_ref, ba_ref, wx_ref, bx_ref, lam_ref, bf_ref,
                y_ref, qa_ref, ka_ref, xx_ref, hprev_ref, cprev_ref):
    tt = lru_ref.shape[0]
    w = LRU_WIDTH

    @pl.when(pl.program_id(1) == 0)
    def _():
        xx_ref[...] = jnp.zeros_like(xx_ref)
        hprev_ref[...] = jnp.zeros_like(hprev_ref)
        cprev_ref[...] = jnp.zeros_like(cprev_ref)

    x = lru_ref[:, :w]
    row8 = lax.broadcasted_iota(jnp.int32, (SUBLANE, w), 0)
    shifts = range(1, CONV_WIDTH)
    before = [pltpu.roll(xx_ref[...], s, 0) for s in shifts]
    groups = []
    for g in range(tt // SUBLANE):
        xg = x[g * SUBLANE:(g + 1) * SUBLANE]
        rolled = [pltpu.roll(xg, s, 0) for s in shifts]
        ug = cb_ref[...] + cw_ref[CONV_WIDTH - 1:CONV_WIDTH, :] * xg
        for s in shifts:
            tap = cw_ref[CONV_WIDTH - 1 - s:CONV_WIDTH - s, :]
            ug = ug + tap * jnp.where(row8 >= s, rolled[s - 1], before[s - 1])
        groups.append(ug)
        before = rolled
    u = jnp.concatenate(groups, axis=0)
    xx_ref[...] = x[tt - SUBLANE:, :]

    ub = u.astype(BF16)
    r = jax.nn.sigmoid(jnp.dot(ub, wa_ref[...], preferred_element_type=F32) + ba_ref[...])
    gi = jax.nn.sigmoid(jnp.dot(ub, wx_ref[...], preferred_element_type=F32) + bx_ref[...])
    log_a = (-LRU_C) * r * _softplus(-lam_ref[...])
    a = jnp.exp(log_a)
    gap = 1.0 - a * a
    b = (gap * lax.rsqrt(jnp.maximum(gap, TINY))) * (gi * u)

    state = hprev_ref[...]
    groups = []
    for g in range(tt // SUBLANE):
        ag = a[g * SUBLANE:(g + 1) * SUBLANE]
        bg = b[g * SUBLANE:(g + 1) * SUBLANE]
        d = 1
        while d < SUBLANE:
            keep = row8 >= d
            bg = ag * jnp.where(keep, pltpu.roll(bg, d, 0), 0.0) + bg
            ag = ag * jnp.where(keep, pltpu.roll(ag, d, 0), 1.0)
            d *= 2
        hg = bg + ag * state
        groups.append(hg)
        state = hg[SUBLANE - 1:, :]
    h = jnp.concatenate(groups, axis=0)
    hprev_ref[...] = state
    y_ref[...] = (h * jax.nn.gelu(lru_ref[:, w:])).astype(y_ref.dtype)

    c = -_softplus(-(f_ref[...] + bf_ref[...]))
    d = 1
    while d < tt:
        c = c + _shift_rows(c, d, 0.0)
        d *= 2
    c = c + cprev_ref[...]
    cprev_ref[...] = c[tt - 1:tt, :]
    lane = lax.broadcasted_iota(jnp.int32, (tt, LANE), 1)
    term = jnp.where(lane >= 3, lane - 3, lane)
    for hd in range(FOX_HEADS):
        ch = jnp.broadcast_to(c[:, hd:hd + 1], (tt, LANE))
        hi = ch.astype(BF16).astype(F32)
        mid = (ch - hi).astype(BF16).astype(F32)
        lo = ((ch - hi) - mid).astype(BF16).astype(F32)
        c3 = jnp.where(term == 0, hi, jnp.where(term == 1, mid, lo))
        cols = slice(hd * LANE, (hd + 1) * LANE)
        qa_ref[:, cols] = jnp.where(lane < 3, c3, jnp.where(lane < 6, 1.0, 0.0)).astype(BF16)
        ka_ref[:, cols] = jnp.where(lane < 3, 1.0, jnp.where(lane < 6, -c3, 0.0)).astype(BF16)


def _lru(lru, f, cw, cb, wa, ba, wx, bx, lam, bf, batch):
    t = lru.shape[0]
    lp = t // batch
    tt = TIME_TILE
    nt = lp // tt
    w = LRU_WIDTH
    row = lambda b, i: (b * nt + i, 0)
    return pl.pallas_call(
        _lru_kernel,
        grid=(batch, nt),
        in_specs=[pl.BlockSpec((tt, 2 * w), row), pl.BlockSpec((tt, LANE), row),
                  _resident(cw.shape), _resident((1, w)), _resident((w, w)), _resident((1, w)),
                  _resident((w, w)), _resident((1, w)), _resident((1, w)), _resident((1, LANE))],
        out_specs=[pl.BlockSpec((tt, w), row), pl.BlockSpec((tt, FOX_WIDTH), row),
                   pl.BlockSpec((tt, FOX_WIDTH), row)],
        out_shape=[jax.ShapeDtypeStruct((t, w), BF16), jax.ShapeDtypeStruct((t, FOX_WIDTH), BF16),
                   jax.ShapeDtypeStruct((t, FOX_WIDTH), BF16)],
        scratch_shapes=[pltpu.VMEM((SUBLANE, w), F32), pltpu.VMEM((1, w), F32), pltpu.VMEM((1, LANE), F32)],
        compiler_params=_params("parallel", "arbitrary"),
        name="lru",
    )(lru, f, cw, cb, wa, ba, wx, bx, lam, bf)


def _wide(a, width):
    return jnp.concatenate([a] * (width // LANE), axis=1)


def _row_norm_max(x):
    x = x.astype(F32)
    return jnp.sqrt(jnp.max(jnp.sum(x * x, axis=1, keepdims=True)))


def _fox_kernel(q_ref, qa_ref, k_ref, ka_ref, v_ref, o_ref, m_ref, acc_ref, kn_ref):
    i = pl.program_id(1)
    tq = q_ref.shape[0]
    tk = KV_CHUNK
    nd = tq // tk
    hd = HEAD_DIM
    ones = jnp.ones((tk, hd), BF16)

    @pl.when(i == 0)
    def _():
        for h in range(FOX_HEADS):
            def blk(r, best, h=h):
                rows = pl.ds(pl.multiple_of(r * tq, tq), tq)
                return jnp.maximum(best, _row_norm_max(k_ref[rows, h * hd:(h + 1) * hd]))
            kn_ref[h] = lax.fori_loop(0, k_ref.shape[0] // tq, blk, jnp.float32(0.0))

    m_ref[...] = jnp.full_like(m_ref, NEG)
    acc_ref[...] = jnp.zeros_like(acc_ref)

    def step(j, r0, frozen=False):
        off = pl.multiple_of(j * tk, tk)
        rows = slice(r0 or 0, tq)
        for h in range(FOX_HEADS):
            cols = slice(h * hd, (h + 1) * hd)
            q2 = jnp.concatenate([q_ref[rows, cols], qa_ref[rows, cols]], axis=1)
            k2 = jnp.concatenate([k_ref[pl.ds(off, tk), cols], ka_ref[pl.ds(off, tk), cols]], axis=1)
            v2 = jnp.concatenate([v_ref[pl.ds(off, tk), cols], ones], axis=1)
            s = lax.dot_general(q2, k2, (((1,), (1,)), ((), ())), preferred_element_type=F32)
            if r0 is not None:
                qpos = i * tq + r0 + lax.broadcasted_iota(jnp.int32, (tq - r0, 1), 0)
                kpos = j * tk + lax.broadcasted_iota(jnp.int32, (1, tk), 1)
                s = jnp.where(kpos <= qpos, s, NEG)
            m_prev = m_ref[h, rows, :]
            if frozen:
                p = jnp.exp(s - _wide(m_prev, tk))
                acc_ref[h, rows, :] += jnp.dot(p.astype(BF16), v2, preferred_element_type=F32)
                continue
            m_new = jnp.maximum(m_prev, jnp.max(s, axis=-1, keepdims=True))
            alpha = jnp.exp(m_prev - m_new)
            p = jnp.exp(s - _wide(m_new, tk))
            acc_ref[h, rows, :] = (_wide(alpha, 2 * hd) * acc_ref[h, rows, :]
                                   + jnp.dot(p.astype(BF16), v2, preferred_element_type=F32))
            m_ref[h, rows, :] = m_new

    for dd in range(nd):
        step(i * nd + dd, dd * tk)

    slack = [_row_norm_max(q_ref[:, h * hd:(h + 1) * hd]) * kn_ref[h] - jnp.min(m_ref[h]) for h in range(FOX_HEADS)]

    def reach(g):
        last = pl.ds(pl.multiple_of((g + 1) * tq - 2 * SUBLANE, 2 * SUBLANE), 2 * SUBLANE)
        top = None
        for h in range(FOX_HEADS):
            cols = slice(h * hd, (h + 1) * hd)
            k_last = ka_ref[last, cols][2 * SUBLANE - 1:, :].astype(F32)
            over = slack[h] + jnp.sum(qa_ref[0:1, cols].astype(F32) * k_last)
            top = over if top is None else jnp.maximum(top, over)
        return top

    def more(state):
        g, top = state
        return jnp.logical_and(g >= 0, top >= -EXP_ZERO)

    def body(state):
        g, top = state

        @pl.when(top <= EXP_SAFE)
        def _():
            for dd in range(nd - 1, -1, -1):
                step(g * nd + dd, None, frozen=True)

        @pl.when(top > EXP_SAFE)
        def _():
            for dd in range(nd - 1, -1, -1):
                step(g * nd + dd, None)
        return g - 1, reach(jnp.maximum(g - 1, 0))

    lax.while_loop(more, body, (i - 1, reach(jnp.maximum(i - 1, 0))))
    for h in range(FOX_HEADS):
        o_ref[:, h * hd:(h + 1) * hd] = (acc_ref[h, :, :hd] / acc_ref[h, :, hd:]).astype(o_ref.dtype)


def _seq_resident(lp, width, col):
    return pl.BlockSpec((lp, width), lambda b, i: (b, col), pipeline_mode=pl.Buffered(1))


def _fox(qkv, qaug, kaug, batch):
    t = qkv.shape[0]
    lp = t // batch
    tq = SEQ_TILE
    nq = lp // tq
    w = FOX_WIDTH
    qmap = lambda b, i: (b * nq + i, 0)
    return pl.pallas_call(
        _fox_kernel,
        grid=(batch, nq),
        in_specs=[pl.BlockSpec((tq, w), qmap), pl.BlockSpec((tq, w), qmap),
                  _seq_resident(lp, w, 1), _seq_resident(lp, w, 0), _seq_resident(lp, w, 2)],
        out_specs=pl.BlockSpec((tq, w), qmap),
        out_shape=jax.ShapeDtypeStruct((t, w), BF16),
        scratch_shapes=[pltpu.VMEM((FOX_HEADS, tq, LANE), F32), pltpu.VMEM((FOX_HEADS, tq, 2 * HEAD_DIM), F32),
                        pltpu.SMEM((FOX_HEADS,), F32)],
        compiler_params=_params("arbitrary", "arbitrary"),
        name="fox",
    )(qkv, qaug, qkv, kaug, qkv)


def _sb_kernel(q_ref, k_ref, v_ref, o_ref, carry_ref, acc_ref):
    i = pl.program_id(1)
    tq = q_ref.shape[0]
    tk = KV_CHUNK
    nd = tq // tk
    hd = HEAD_DIM
    rr = lax.broadcasted_iota(jnp.int32, (tk, tk), 0)
    cc = lax.broadcasted_iota(jnp.int32, (tk, tk), 1)
    later_mat = jnp.where(rr > cc, 1.0, 0.0).astype(BF16)
    carry_ref[...] = jnp.zeros_like(carry_ref)
    acc_ref[...] = jnp.zeros_like(acc_ref)

    def step(j, r0):
        off = pl.multiple_of(j * tk, tk)
        rows = slice(r0 or 0, tq)
        for h in range(SB_HEADS):
            cols = slice(h * hd, (h + 1) * hd)
            z = lax.dot_general(q_ref[rows, cols], k_ref[pl.ds(off, tk), cols], (((1,), (1,)), ((), ())),
                                preferred_element_type=F32)
            sp = _softplus_scores(z)
            spm = sp
            if r0 is not None:
                qpos = i * tq + r0 + lax.broadcasted_iota(jnp.int32, (tq - r0, 1), 0)
                mask = (j * tk + lax.broadcasted_iota(jnp.int32, (1, tk), 1)) < qpos
                spm = jnp.where(mask, sp, 0.0)
            cum = (jnp.dot(spm.astype(BF16), later_mat, preferred_element_type=F32)
                   + _wide(carry_ref[h, rows, :], tk))
            a = jnp.exp((z - sp) - cum)
            if r0 is not None:
                a = jnp.where(mask, a, 0.0)
            acc_ref[h, rows, :] += jnp.dot(a.astype(BF16), v_ref[pl.ds(off, tk), cols], preferred_element_type=F32)
            carry_ref[h, rows, :] = jnp.broadcast_to(cum[:, 0:1] + spm[:, 0:1], (tq - rows.start, LANE))

    for dd in range(nd - 1, -1, -1):
        step(i * nd + dd, dd * tk)

    def least_carry():
        least = carry_ref[0]
        for h in range(1, SB_HEADS):
            least = jnp.minimum(least, carry_ref[h])
        return jnp.min(least)

    def more(state):
        j, least = state
        return jnp.logical_and(j >= 0, least < EXP_ZERO)

    def body(state):
        j, _ = state
        step(j, None)
        return j - 1, least_carry()

    lax.while_loop(more, body, (i * nd - 1, least_carry()))
    for h in range(SB_HEADS):
        o_ref[:, h * hd:(h + 1) * hd] = acc_ref[h].astype(o_ref.dtype)


def _sb(qkv, batch):
    t = qkv.shape[0]
    lp = t // batch
    tq = SEQ_TILE
    nq = lp // tq
    w = SB_WIDTH
    base = 3 * FOX_WIDTH // w
    return pl.pallas_call(
        _sb_kernel,
        grid=(batch, nq),
        in_specs=[pl.BlockSpec((tq, w), lambda b, i: (b * nq + i, base)),
                  _seq_resident(lp, w, base + 1), _seq_resident(lp, w, base + 2)],
        out_specs=pl.BlockSpec((tq, w), lambda b, i: (b * nq + i, 0)),
        out_shape=jax.ShapeDtypeStruct((t, w), BF16),
        scratch_shapes=[pltpu.VMEM((SB_HEADS, tq, LANE), F32), pltpu.VMEM((SB_HEADS, tq, HEAD_DIM), F32)],
        compiler_params=_params("parallel", "arbitrary"),
        name="stickbreak",
    )(qkv, qkv, qkv)


def _merge_kernel(h_ref, g_ref, wg0_ref, wg1_ref, wg2_ref, wgt_ref, bgate_ref, yf_ref, ys_ref, yl_ref,
                  wfo_ref, wso_ref, wlo_ref, wout_ref, o_ref, wgate_ref):
    @pl.when(pl.program_id(0) == 0)
    def _():
        blocks = (wg0_ref, wg1_ref, wg2_ref, wgt_ref)
        dd = h_ref.shape[1]
        for br in range(N_BRANCH):
            both = jnp.concatenate([blocks[br][...], blocks[br + 1][:, :LANE]], axis=1)
            wgate_ref[:, br * dd:(br + 1) * dd] = both[:, GATE_SHIFT:GATE_SHIFT + dd].astype(BF16)

    h = h_ref[...]
    d = h.shape[1]
    xn = _rms(h, g_ref[...]).astype(BF16)
    merged = None
    for br, (y_ref, w_ref) in enumerate(((yf_ref, wfo_ref), (ys_ref, wso_ref), (yl_ref, wlo_ref))):
        cols = slice(br * d, (br + 1) * d)
        gate = jax.nn.sigmoid(jnp.dot(xn, wgate_ref[:, cols], preferred_element_type=F32) + bgate_ref[:, cols])
        term = gate * jnp.dot(y_ref[...], w_ref[...], preferred_element_type=F32)
        merged = term if merged is None else merged + term
    o_ref[...] = h + jnp.dot(merged.astype(BF16), wout_ref[...], preferred_element_type=F32)


def _merge(h, g, w_in, layer, bgate, yf, ys, yl, wfo, wso, wlo, wout):
    t, d = h.shape
    tm = ROW_TILE
    row = lambda i: (i, 0)
    first = COL_GATE - GATE_SHIFT
    assert first % d == 0
    wspec = lambda width, blk: pl.BlockSpec((None, d, width), lambda i: (layer, 0, blk),
                                            pipeline_mode=pl.Buffered(1))
    return pl.pallas_call(
        _merge_kernel,
        grid=(t // tm,),
        in_specs=[pl.BlockSpec((tm, d), row), _resident((1, d)),
                  wspec(d, first // d), wspec(d, first // d + 1), wspec(d, first // d + 2),
                  wspec(LANE, (first + N_BRANCH * d) // LANE), _resident(bgate.shape),
                  pl.BlockSpec((tm, yf.shape[1]), row), pl.BlockSpec((tm, ys.shape[1]), row),
                  pl.BlockSpec((tm, yl.shape[1]), row),
                  _resident(wfo.shape), _resident(wso.shape), _resident(wlo.shape), _resident(wout.shape)],
        out_specs=pl.BlockSpec((tm, d), row),
        out_shape=jax.ShapeDtypeStruct((t, d), F32),
        scratch_shapes=[pltpu.VMEM((d, N_BRANCH * d), BF16)],
        compiler_params=_params("arbitrary"),
        name="merge",
    )(h, g, w_in, w_in, w_in, w_in, bgate, yf, ys, yl, wfo, wso, wlo, wout)


def _ffn_chunk(ff):
    best = LANE
    for c in range(LANE, 1536 + 1, LANE):
        if ff % c == 0:
            best = c
    return best


def _ffn_kernel(h_ref, g_ref, wg_ref, wu_ref, wd_ref, gfin_ref, o_ref, *, final):
    h = h_ref[...]
    hn = _rms(h, g_ref[...]).astype(BF16)
    ff = wg_ref.shape[1]
    fc = _ffn_chunk(ff)
    out = h
    for c in range(ff // fc):
        cols = slice(c * fc, (c + 1) * fc)
        act = jax.nn.silu(jnp.dot(hn, wg_ref[:, cols], preferred_element_type=F32))
        act = act * jnp.dot(hn, wu_ref[:, cols], preferred_element_type=F32)
        out = out + jnp.dot(act.astype(BF16), wd_ref[cols, :], preferred_element_type=F32)
    if final:
        out = _rms(out, gfin_ref[...])
    o_ref[...] = out


def _ffn(h, g, wg, wu, wd, gfin, final):
    t, d = h.shape
    tm = ROW_TILE
    row = lambda i: (i, 0)
    return pl.pallas_call(
        functools.partial(_ffn_kernel, final=final),
        grid=(t // tm,),
        in_specs=[pl.BlockSpec((tm, d), row), _resident((1, d)), _resident(wg.shape), _resident(wu.shape),
                  _resident(wd.shape), _resident((1, d))],
        out_specs=pl.BlockSpec((tm, d), row),
        out_shape=jax.ShapeDtypeStruct((t, d), F32),
        compiler_params=_params("parallel"),
        name="ffn",
    )(h, g, wg, wu, wd, gfin)


def _router_kernel(h_ref, g_ref, wr_ref, hn_ref, idx_ref, wts_ref, cnt_ref, run_ref):
    tm = h_ref.shape[0]

    @pl.when(pl.program_id(0) == 0)
    def _():
        run_ref[...] = jnp.zeros_like(run_ref)

    hn = _rms(h_ref[...], g_ref[...])
    hn_ref[...] = hn
    logits = jnp.dot(hn, wr_ref[...], preferred_element_type=F32, precision=lax.Precision.HIGHEST)
    lane = lax.broadcasted_iota(jnp.int32, (tm, LANE), 1)
    lg = jnp.where(lane < N_EXPERTS, logits, -jnp.inf)
    v1 = jnp.max(lg, axis=-1, keepdims=True)
    i1 = jnp.min(jnp.where(lg == v1, lane, LANE), axis=-1, keepdims=True)
    lg2 = jnp.where(lane == i1, -jnp.inf, lg)
    v2 = jnp.max(lg2, axis=-1, keepdims=True)
    i2 = jnp.min(jnp.where(lg2 == v2, lane, LANE), axis=-1, keepdims=True)
    e = jnp.exp(v2 - v1)
    w1 = 1.0 / (1.0 + e)
    w2 = e * w1
    sel1 = lane == i1
    sel2 = lane == i2
    onehot = jnp.where(sel1, 1.0, 0.0) + jnp.where(sel2, 1.0, 0.0)
    rr = lax.broadcasted_iota(jnp.int32, (tm, tm), 0)
    cc = lax.broadcasted_iota(jnp.int32, (tm, tm), 1)
    before = jnp.where(cc < rr, 1.0, 0.0).astype(BF16)
    excl = jnp.dot(before, onehot.astype(BF16), preferred_element_type=F32) + run_ref[...]
    r1 = jnp.sum(jnp.where(sel1, excl, 0.0), axis=-1, keepdims=True)
    r2 = jnp.sum(jnp.where(sel2, excl, 0.0), axis=-1, keepdims=True)
    run = run_ref[...] + jnp.sum(onehot, axis=0, keepdims=True)
    run_ref[...] = run
    idx = jnp.where(lane == 0, i1, jnp.where(lane == 1, i2, 0))
    rank = jnp.where(lane == 2, r1, jnp.where(lane == 3, r2, 0.0)).astype(jnp.int32)
    idx_ref[...] = idx + rank
    wts_ref[...] = jnp.where(lane == 0, w1, jnp.where(lane == 1, w2, 0.0))
    cnt_ref[...] = jnp.broadcast_to(run, cnt_ref.shape)


def _router(h, g, wr):
    t, d = h.shape
    tm = ROW_TILE
    row = lambda i: (i, 0)
    return pl.pallas_call(
        _router_kernel,
        grid=(t // tm,),
        in_specs=[pl.BlockSpec((tm, d), row), _resident((1, d)), _resident(wr.shape)],
        out_specs=[pl.BlockSpec((tm, d), row), pl.BlockSpec((tm, LANE), row), pl.BlockSpec((tm, LANE), row),
                   pl.BlockSpec((SUBLANE, LANE), lambda i: (0, 0))],
        out_shape=[jax.ShapeDtypeStruct((t, d), F32), jax.ShapeDtypeStruct((t, LANE), jnp.int32),
                   jax.ShapeDtypeStruct((t, LANE), F32), jax.ShapeDtypeStruct((SUBLANE, LANE), F32)],
        scratch_shapes=[pltpu.VMEM((1, LANE), F32)],
        compiler_params=_params("arbitrary"),
        name="router",
    )(h, g, wr)


def _row_copy(src, s, dst, d, sem):
    return pltpu.make_async_copy(src.at[pl.ds(s, 1)], dst.at[pl.ds(d, 1)], sem)


def _dispatch_kernel(pos_ref, tails_ref, hn_ref, xs_hbm, zero_ref, sem):
    i = pl.program_id(0)
    tm = hn_ref.shape[0]

    @pl.when(i == 0)
    def _():
        zero_ref[...] = jnp.zeros_like(zero_ref)
        tz = zero_ref.shape[0]
        tails = [pl.ds(pl.multiple_of(tails_ref[e], SUBLANE), tz) for e in range(N_EXPERTS)]
        tails += [pl.ds(xs_hbm.shape[0] - (e + 1) * tz, tz) for e in range(N_EXPERTS)]
        for rows in tails:
            clear = pltpu.make_async_copy(zero_ref, xs_hbm.at[rows], sem)
            clear.start()
            clear.wait()

    def issue(g, carry):
        for u in range(ISSUE_UNROLL):
            r = g * ISSUE_UNROLL + u
            _row_copy(hn_ref, r, xs_hbm, pos_ref[i, r], sem).start()
            _row_copy(hn_ref, r, xs_hbm, pos_ref[i, tm + r], sem).start()
        return carry

    lax.fori_loop(0, tm // ISSUE_UNROLL, issue, 0)
    for _ in range(2):
        pltpu.make_async_copy(hn_ref, xs_hbm.at[pl.ds(0, tm)], sem).wait()


def _dispatch(pos, tails, hn, n_rows):
    nt = pos.shape[0]
    t, d = hn.shape
    return pl.pallas_call(
        _dispatch_kernel,
        grid_spec=pltpu.PrefetchScalarGridSpec(
            num_scalar_prefetch=2, grid=(nt,),
            in_specs=[pl.BlockSpec((t // nt, d), lambda i, p, tl: (i, 0))],
            out_specs=pl.BlockSpec(memory_space=pl.ANY),
            scratch_shapes=[pltpu.VMEM((MOE_TILE, d), F32), pltpu.SemaphoreType.DMA(())]),
        out_shape=jax.ShapeDtypeStruct((n_rows, d), F32),
        compiler_params=pltpu.CompilerParams(dimension_semantics=("arbitrary",), has_side_effects=True,
                                             vmem_limit_bytes=VMEM_LIMIT, disable_bounds_checks=True),
        name="dispatch",
    )(pos, tails, hn)


def _experts_kernel(te_ref, na_ref, xs_ref, wg_ref, wu_ref, wd_ref, ys_ref, xb_ref, acc_ref):
    del te_ref
    i = pl.program_id(0)
    f = pl.program_id(1)

    @pl.when(f == 0)
    def _():
        xb_ref[...] = xs_ref[...].astype(BF16)
        acc_ref[...] = jnp.zeros_like(acc_ref)

    @pl.when(i < na_ref[0])
    def _():
        xb = xb_ref[...]
        act = jax.nn.silu(jnp.dot(xb, wg_ref[...], preferred_element_type=F32))
        act = act * jnp.dot(xb, wu_ref[...], preferred_element_type=F32)
        acc_ref[...] += jnp.dot(act.astype(BF16), wd_ref[...], preferred_element_type=F32)

    @pl.when(f == pl.num_programs(1) - 1)
    def _():
        ys_ref[...] = acc_ref[...]


def _experts(tile_expert, n_active, xs, wg, wu, wd):
    nr, d = xs.shape
    tm = MOE_TILE
    ff = wg.shape[2]
    tf = MOE_FF_TILE if ff % MOE_FF_TILE == 0 else ff
    return pl.pallas_call(
        _experts_kernel,
        grid_spec=pltpu.PrefetchScalarGridSpec(
            num_scalar_prefetch=2, grid=(nr // tm, ff // tf),
            in_specs=[pl.BlockSpec((tm, d), lambda i, f, te, na: (jnp.minimum(i, na[0] - 1), 0)),
                      pl.BlockSpec((None, d, tf), lambda i, f, te, na: (te[i], 0, f)),
                      pl.BlockSpec((None, d, tf), lambda i, f, te, na: (te[i], 0, f)),
                      pl.BlockSpec((None, tf, d), lambda i, f, te, na: (te[i], f, 0))],
            out_specs=pl.BlockSpec((tm, d), lambda i, f, te, na: (i, 0)),
            scratch_shapes=[pltpu.VMEM((tm, d), BF16), pltpu.VMEM((tm, d), F32)]),
        out_shape=jax.ShapeDtypeStruct((nr, d), F32),
        compiler_params=_params("arbitrary", "arbitrary"),
        name="experts",
    )(tile_expert, n_active, xs, wg, wu, wd)


def _combine_kernel(pos_ref, h_ref, wts_ref, gfin_ref, ys_hbm, o_ref, buf_ref, sem, *, final):
    i = pl.program_id(0)
    tm = h_ref.shape[0]

    def gather(step, slot):
        def issue(g, carry):
            for u in range(ISSUE_UNROLL):
                r = g * ISSUE_UNROLL + u
                _row_copy(ys_hbm, pos_ref[step, r], buf_ref.at[slot], r, sem.at[slot]).start()
                _row_copy(ys_hbm, pos_ref[step, tm + r], buf_ref.at[slot], tm + r, sem.at[slot]).start()
            return carry
        lax.fori_loop(0, tm // ISSUE_UNROLL, issue, 0)

    slot = i % 2

    @pl.when(i == 0)
    def _():
        gather(0, 0)

    @pl.when(i + 1 < pl.num_programs(0))
    def _():
        gather(i + 1, 1 - slot)

    pltpu.make_async_copy(ys_hbm.at[pl.ds(0, 2 * tm)], buf_ref.at[slot], sem.at[slot]).wait()
    wts = wts_ref[...]
    out = h_ref[...] + wts[:, 0:1] * buf_ref[slot, 0:tm, :] + wts[:, 1:2] * buf_ref[slot, tm:2 * tm, :]
    if final:
        out = _rms(out, gfin_ref[...])
    o_ref[...] = out


def _combine(pos, h, wts, gfin, ys, final):
    t, d = h.shape
    nt = pos.shape[0]
    tm = t // nt
    row = lambda i, p: (i, 0)
    return pl.pallas_call(
        functools.partial(_combine_kernel, final=final),
        grid_spec=pltpu.PrefetchScalarGridSpec(
            num_scalar_prefetch=1, grid=(nt,),
            in_specs=[pl.BlockSpec((tm, d), row), pl.BlockSpec((tm, LANE), row),
                      pl.BlockSpec((1, d), lambda i, p: (0, 0)), pl.BlockSpec(memory_space=pl.ANY)],
            out_specs=pl.BlockSpec((tm, d), row),
            scratch_shapes=[pltpu.VMEM((2, 2 * tm, d), F32), pltpu.SemaphoreType.DMA((2,))]),
        out_shape=jax.ShapeDtypeStruct((t, d), F32),
        compiler_params=pltpu.CompilerParams(dimension_semantics=("arbitrary",), vmem_limit_bytes=VMEM_LIMIT,
                                             disable_bounds_checks=True),
        name="combine",
    )(pos, h, wts, gfin, ys)


def _moe(h, g, router_w, wg, wu, wd, gfin, final):
    t, d = h.shape
    tm = MOE_TILE
    wr = jnp.pad(router_w, ((0, 0), (0, LANE - N_EXPERTS)))
    hn, idx, wts, cnt = _router(h, g, wr)
    counts = cnt[0, :N_EXPERTS].astype(jnp.int32)
    padded = ((counts + tm - 1) // tm) * tm
    ends = jnp.cumsum(padded)
    offs = ends - padded
    n_tiles = (2 * t + N_EXPERTS * (tm - 1)) // tm
    pos1 = jnp.take(offs, idx[:, 0]) + idx[:, 2]
    pos2 = jnp.take(offs, idx[:, 1]) + idx[:, 3]
    pos = jnp.concatenate([pos1.reshape(-1, ROW_TILE), pos2.reshape(-1, ROW_TILE)], axis=1)
    tile_start = jnp.arange(n_tiles, dtype=jnp.int32) * tm
    tile_expert = jnp.minimum(jnp.sum(tile_start[:, None] >= ends[None, :], axis=1), N_EXPERTS - 1).astype(jnp.int32)
    n_active = (ends[-1:] // tm).astype(jnp.int32)
    tails = jnp.where(padded > 0, ends - tm, 0).astype(jnp.int32)
    xs = _dispatch(pos, tails, hn, n_tiles * tm)
    ys = _experts(tile_expert, n_active, xs, wg, wu, wd)
    return _combine(pos, h, wts, gfin, ys, final)


def _block_diag(w):
    n, a, b = w.shape
    eye = jnp.eye(n, dtype=w.dtype)
    return jnp.einsum('nij,nm->nimj', w, eye).reshape(n * a, n * b)


def kernel(x, meta_tokens, g_mix, w_in, b_forget, b_gate, conv_w, conv_b, lru_wa, lru_ba, lru_wx, lru_bx, lru_lambda, w_fox_o, w_sb_o, w_lru_o, w_out, g_ffn, ffn_w_gate, ffn_w_up, ffn_w_down, router_w, moe_w_gate, moe_w_up, moe_w_down, g_final):
    batch, seq, d = x.shape
    depth = g_mix.shape[0]
    l = seq + N_META
    lp = -(-l // SEQ_TILE) * SEQ_TILE
    assert (batch * lp) % ROW_TILE == 0 and lp % TIME_TILE == 0 and SEQ_TILE % KV_CHUNK == 0
    meta = jnp.broadcast_to(meta_tokens[None].astype(x.dtype), (batch, N_META, d))
    h = jnp.concatenate([meta, x, jnp.zeros((batch, lp - l, d), x.dtype)], axis=1).reshape(batch * lp, d)

    gfin = g_final.reshape(1, d)
    for layer in range(depth):
        bf = jnp.pad(b_forget[layer], (0, LANE - FOX_HEADS)).reshape(1, LANE)

        qkv, lru, f = _inproj(h, g_mix[layer].reshape(1, d), w_in, layer)
        yl, qaug, kaug = _lru(lru, f, conv_w[layer], conv_b[layer].reshape(1, -1),
                              _block_diag(lru_wa[layer]).astype(BF16), lru_ba[layer].reshape(1, -1),
                              _block_diag(lru_wx[layer]).astype(BF16), lru_bx[layer].reshape(1, -1),
                              lru_lambda[layer].reshape(1, -1), bf, batch)
        yf = _fox(qkv, qaug, kaug, batch)
        ys = _sb(qkv, batch)
        h = _merge(h, g_mix[layer].reshape(1, d), w_in, layer, b_gate[layer].reshape(1, -1), yf, ys, yl,
                   w_fox_o[layer].astype(BF16), w_sb_o[layer].astype(BF16), w_lru_o[layer].astype(BF16),
                   w_out[layer].astype(BF16))

        final = layer == depth - 1
        j = layer // 2
        gf = g_ffn[layer].reshape(1, d)
        if layer % 2 == 0:
            h = _ffn(h, gf, ffn_w_gate[j].astype(BF16), ffn_w_up[j].astype(BF16), ffn_w_down[j].astype(BF16),
                     gfin, final)
        else:
            h = _moe(h, gf, router_w[j], moe_w_gate[j].astype(BF16), moe_w_up[j].astype(BF16),
                     moe_w_down[j].astype(BF16), gfin, final)
    return h.reshape(batch, lp, d)[:, N_META:l]
```

```python
import functools

import jax
import jax.numpy as jnp
from jax import lax
from jax.experimental import pallas as pl
from jax.experimental.pallas import tpu as pltpu

N_META = 16
HEAD_DIM = 128
FOX_HEADS = 4
SB_HEADS = 4
FOX_WIDTH = FOX_HEADS * HEAD_DIM
SB_WIDTH = SB_HEADS * HEAD_DIM
LRU_WIDTH = 512
LRU_BLOCKS = 8
LRU_C = 8.0
CONV_WIDTH = 4
N_BRANCH = 3
N_EXPERTS = 8
RMS_EPS = 1e-6
NEG = -1e30
SOFTPLUS_LINEAR = 40.0
TINY = 1e-30
EXP_ZERO = 105.0
EXP_SAFE = 40.0
COL_F = 3 * FOX_WIDTH
COL_SB = COL_F + FOX_HEADS
COL_LRU = COL_SB + 3 * SB_WIDTH
COL_GATE = COL_LRU + 2 * LRU_WIDTH

LANE = 128
SUBLANE = 8
SEQ_TILE = 768
KV_CHUNK = 256
ROW_TILE = 512
TIME_TILE = 256
MOE_TILE = 512
MOE_FF_TILE = 1792
ISSUE_UNROLL = 8
VMEM_LIMIT = 56 * 1024 * 1024
GATE_SHIFT = COL_GATE % LANE

F32 = jnp.float32
BF16 = jnp.bfloat16


def _params(*sem):
    return pltpu.CompilerParams(dimension_semantics=sem, vmem_limit_bytes=VMEM_LIMIT)


def _resident(shape):
    return pl.BlockSpec(shape, lambda *_: (0,) * len(shape), pipeline_mode=pl.Buffered(1))


def _rms(x, g):
    return x * lax.rsqrt(jnp.mean(x * x, axis=-1, keepdims=True) + RMS_EPS) * g


def _softplus(x):
    return jnp.maximum(x, 0.0) + jnp.log1p(jnp.exp(-jnp.abs(x)))


def _softplus_scores(x):
    return jnp.where(x > SOFTPLUS_LINEAR, x, jnp.log(1.0 + jnp.exp(x)))


def _inproj_kernel(h_ref, g_ref, w32_ref, qkv_ref, lru_ref, f_ref, wqkv_ref, wlru_ref, wf_ref):
    @pl.when(pl.program_id(0) == 0)
    def _():
        wqkv_ref[:, :COL_F] = w32_ref[:, :COL_F].astype(BF16)
        for c in range(3):
            src = slice(COL_SB + c * SB_WIDTH, COL_SB + (c + 1) * SB_WIDTH)
            wqkv_ref[:, COL_F + c * SB_WIDTH:COL_F + (c + 1) * SB_WIDTH] = w32_ref[:, src].astype(BF16)
        wlru_ref[...] = w32_ref[:, COL_LRU:COL_GATE].astype(BF16)
        wf_ref[...] = w32_ref[:, COL_F:COL_F + LANE].astype(BF16)

    xn = _rms(h_ref[...], g_ref[...]).astype(BF16)
    scale = HEAD_DIM ** -0.5
    for c in range(6):
        cols = slice(c * FOX_WIDTH, (c + 1) * FOX_WIDTH)
        r = jnp.dot(xn, wqkv_ref[:, cols], preferred_element_type=F32)
        if c % 3 == 0:
            r = r * scale
        qkv_ref[:, cols] = r.astype(BF16)
    lru_ref[...] = jnp.dot(xn, wlru_ref[...], preferred_element_type=F32)
    f_ref[...] = jnp.dot(xn, wf_ref[...], preferred_element_type=F32)


def _inproj(h, g, w_in, layer):
    t, d = h.shape
    tm = ROW_TILE
    nqkv = 3 * (FOX_WIDTH + SB_WIDTH)
    nlru = 2 * LRU_WIDTH
    wblock = -(-COL_GATE // LANE) * LANE
    return pl.pallas_call(
        _inproj_kernel,
        grid=(t // tm,),
        in_specs=[pl.BlockSpec((tm, d), lambda i: (i, 0)), _resident((1, d)),
                  pl.BlockSpec((None, d, wblock), lambda i: (layer, 0, 0), pipeline_mode=pl.Buffered(1))],
        out_specs=[pl.BlockSpec((tm, nqkv), lambda i: (i, 0)), pl.BlockSpec((tm, nlru), lambda i: (i, 0)),
                   pl.BlockSpec((tm, LANE), lambda i: (i, 0))],
        out_shape=[jax.ShapeDtypeStruct((t, nqkv), BF16), jax.ShapeDtypeStruct((t, nlru), F32),
                   jax.ShapeDtypeStruct((t, LANE), F32)],
        scratch_shapes=[pltpu.VMEM((d, nqkv), BF16), pltpu.VMEM((d, nlru), BF16), pltpu.VMEM((d, LANE), BF16)],
        compiler_params=_params("arbitrary"),
        name="inproj",
    )(h, g, w_in)


def _shift_rows(a, d, fill):
    rows = lax.broadcasted_iota(jnp.int32, a.shape, 0)
    return jnp.where(rows >= d, pltpu.roll(a, d, 0), fill)


def _lru_kernel(lru_ref, f_ref, cw_ref, cb_ref, wa_ref, ba_ref, wx_ref, bx_ref, lam_ref, bf_ref,
                y_ref, qa_ref, ka_ref, xx_ref, hprev_ref, cprev_ref):
    tt = lru_ref.shape[0]
    w = LRU_WIDTH

    @pl.when(pl.program_id(1) == 0)
    def _():
        xx_ref[...] = jnp.zeros_like(xx_ref)
        hprev_ref[...] = jnp.zeros_like(hprev_ref)
        cprev_ref[...] = jnp.zeros_like(cprev_ref)

    x = lru_ref[:, :w]
    row8 = lax.broadcasted_iota(jnp.int32, (SUBLANE, w), 0)
    shifts = range(1, CONV_WIDTH)
    before = [pltpu.roll(xx_ref[...], s, 0) for s in shifts]
    groups = []
    for g in range(tt // SUBLANE):
        xg = x[g * SUBLANE:(g + 1) * SUBLANE]
        rolled = [pltpu.roll(xg, s, 0) for s in shifts]
        ug = cb_ref[...] + cw_ref[CONV_WIDTH - 1:CONV_WIDTH, :] * xg
        for s in shifts:
            tap = cw_ref[CONV_WIDTH - 1 - s:CONV_WIDTH - s, :]
            ug = ug + tap * jnp.where(row8 >= s, rolled[s - 1], before[s - 1])
        groups.append(ug)
        before = rolled
    u = jnp.concatenate(groups, axis=0)
    xx_ref[...] = x[tt - SUBLANE:, :]

    ub = u.astype(BF16)
    r = jax.nn.sigmoid(jnp.dot(ub, wa_ref[...], preferred_element_type=F32) + ba_ref[...])
    gi = jax.nn.sigmoid(jnp.dot(ub, wx_ref[...], preferred_element_type=F32) + bx_ref[...])
    log_a = (-LRU_C) * r * _softplus(-lam_ref[...])
    a = jnp.exp(log_a)
    gap = 1.0 - a * a
    b = (gap * lax.rsqrt(jnp.maximum(gap, TINY))) * (gi * u)

    state = hprev_ref[...]
    groups = []
    for g in range(tt // SUBLANE):
        ag = a[g * SUBLANE:(g + 1) * SUBLANE]
        bg = b[g * SUBLANE:(g + 1) * SUBLANE]
        d = 1
        while d < SUBLANE:
            keep = row8 >= d
            bg = ag * jnp.where(keep, pltpu.roll(bg, d, 0), 0.0) + bg
            ag = ag * jnp.where(keep, pltpu.roll(ag, d, 0), 1.0)
            d *= 2
        hg = bg + ag * state
        groups.append(hg)
        state = hg[SUBLANE - 1:, :]
    h = jnp.concatenate(groups, axis=0)
    hprev_ref[...] = state
    y_ref[...] = (h * jax.nn.gelu(lru_ref[:, w:])).astype(y_ref.dtype)

    c = -_softplus(-(f_ref[...] + bf_ref[...]))
    d = 1
    while d < tt:
        c = c + _shift_rows(c, d, 0.0)
        d *= 2
    c = c + cprev_ref[...]
    cprev_ref[...] = c[tt - 1:tt, :]
    lane = lax.broadcasted_iota(jnp.int32, (tt, LANE), 1)
    term = jnp.where(lane >= 3, lane - 3, lane)
    for hd in range(FOX_HEADS):
        ch = jnp.broadcast_to(c[:, hd:hd + 1], (tt, LANE))
        hi = ch.astype(BF16).astype(F32)
        mid = (ch - hi).astype(BF16).astype(F32)
        lo = ((ch - hi) - mid).astype(BF16).astype(F32)
        c3 = jnp.where(term == 0, hi, jnp.where(term == 1, mid, lo))
        cols = slice(hd * LANE, (hd + 1) * LANE)
        qa_ref[:, cols] = jnp.where(lane < 3, c3, jnp.where(lane < 6, 1.0, 0.0)).astype(BF16)
        ka_ref[:, cols] = jnp.where(lane < 3, 1.0, jnp.where(lane < 6, -c3, 0.0)).astype(BF16)


def _lru(lru, f, cw, cb, wa, ba, wx, bx, lam, bf, batch):
    t = lru.shape[0]
    lp = t // batch
    tt = TIME_TILE
    nt = lp // tt
    w = LRU_WIDTH
    row = lambda b, i: (b * nt + i, 0)
    return pl.pallas_call(
        _lru_kernel,
        grid=(batch, nt),
        in_specs=[pl.BlockSpec((tt, 2 * w), row), pl.BlockSpec((tt, LANE), row),
                  _resident(cw.shape), _resident((1, w)), _resident((w, w)), _resident((1, w)),
                  _resident((w, w)), _resident((1, w)), _resident((1, w)), _resident((1, LANE))],
        out_specs=[pl.BlockSpec((tt, w), row), pl.BlockSpec((tt, FOX_WIDTH), row),
                   pl.BlockSpec((tt, FOX_WIDTH), row)],
        out_shape=[jax.ShapeDtypeStruct((t, w), BF16), jax.ShapeDtypeStruct((t, FOX_WIDTH), BF16),
                   jax.ShapeDtypeStruct((t, FOX_WIDTH), BF16)],
        scratch_shapes=[pltpu.VMEM((SUBLANE, w), F32), pltpu.VMEM((1, w), F32), pltpu.VMEM((1, LANE), F32)],
        compiler_params=_params("parallel", "arbitrary"),
        name="lru",
    )(lru, f, cw, cb, wa, ba, wx, bx, lam, bf)


def _wide(a, width):
    return jnp.concatenate([a] * (width // LANE), axis=1)


def _row_norm_max(x):
    x = x.astype(F32)
    return jnp.sqrt(jnp.max(jnp.sum(x * x, axis=1, keepdims=True)))


def _fox_kernel(q_ref, qa_ref, k_ref, ka_ref, v_ref, o_ref, m_ref, acc_ref, kn_ref):
    i = pl.program_id(1)
    tq = q_ref.shape[0]
    tk = KV_CHUNK
    nd = tq // tk
    hd = HEAD_DIM
    ones = jnp.ones((tk, hd), BF16)

    @pl.when(i == 0)
    def _():
        for h in range(FOX_HEADS):
            def blk(r, best, h=h):
                rows = pl.ds(pl.multiple_of(r * tq, tq), tq)
                return jnp.maximum(best, _row_norm_max(k_ref[rows, h * hd:(h + 1) * hd]))
            kn_ref[h] = lax.fori_loop(0, k_ref.shape[0] // tq, blk, jnp.float32(0.0))

    m_ref[...] = jnp.full_like(m_ref, NEG)
    acc_ref[...] = jnp.zeros_like(acc_ref)

    def step(j, r0, frozen=False):
        off = pl.multiple_of(j * tk, tk)
        rows = slice(r0 or 0, tq)
        for h in range(FOX_HEADS):
            cols = slice(h * hd, (h + 1) * hd)
            q2 = jnp.concatenate([q_ref[rows, cols], qa_ref[rows, cols]], axis=1)
            k2 = jnp.concatenate([k_ref[pl.ds(off, tk), cols], ka_ref[pl.ds(off, tk), cols]], axis=1)
            v2 = jnp.concatenate([v_ref[pl.ds(off, tk), cols], ones], axis=1)
            s = lax.dot_general(q2, k2, (((1,), (1,)), ((), ())), preferred_element_type=F32)
            if r0 is not None:
                qpos = i * tq + r0 + lax.broadcasted_iota(jnp.int32, (tq - r0, 1), 0)
                kpos = j * tk + lax.broadcasted_iota(jnp.int32, (1, tk), 1)
                s = jnp.where(kpos <= qpos, s, NEG)
            m_prev = m_ref[h, rows, :]
            if frozen:
                p = jnp.exp(s - _wide(m_prev, tk))
                acc_ref[h, rows, :] += jnp.dot(p.astype(BF16), v2, preferred_element_type=F32)
                continue
            m_new = jnp.maximum(m_prev, jnp.max(s, axis=-1, keepdims=True))
            alpha = jnp.exp(m_prev - m_new)
            p = jnp.exp(s - _wide(m_new, tk))
            acc_ref[h, rows, :] = (_wide(alpha, 2 * hd) * acc_ref[h, rows, :]
                                   + jnp.dot(p.astype(BF16), v2, preferred_element_type=F32))
            m_ref[h, rows, :] = m_new

    for dd in range(nd):
        step(i * nd + dd, dd * tk)

    slack = [_row_norm_max(q_ref[:, h * hd:(h + 1) * hd]) * kn_ref[h] - jnp.min(m_ref[h]) for h in range(FOX_HEADS)]

    def reach(g):
        last = pl.ds(pl.multiple_of((g + 1) * tq - 2 * SUBLANE, 2 * SUBLANE), 2 * SUBLANE)
        top = None
        for h in range(FOX_HEADS):
            cols = slice(h * hd, (h + 1) * hd)
            k_last = ka_ref[last, cols][2 * SUBLANE - 1:, :].astype(F32)
            over = slack[h] + jnp.sum(qa_ref[0:1, cols].astype(F32) * k_last)
            top = over if top is None else jnp.maximum(top, over)
        return top

    def more(state):
        g, top = state
        return jnp.logical_and(g >= 0, top >= -EXP_ZERO)

    def body(state):
        g, top = state

        @pl.when(top <= EXP_SAFE)
        def _():
            for dd in range(nd - 1, -1, -1):
                step(g * nd + dd, None, frozen=True)

        @pl.when(top > EXP_SAFE)
        def _():
            for dd in range(nd - 1, -1, -1):
                step(g * nd + dd, None)
        return g - 1, reach(jnp.maximum(g - 1, 0))

    lax.while_loop(more, body, (i - 1, reach(jnp.maximum(i - 1, 0))))
    for h in range(FOX_HEADS):
        o_ref[:, h * hd:(h + 1) * hd] = (acc_ref[h, :, :hd] / acc_ref[h, :, hd:]).astype(o_ref.dtype)


def _seq_resident(lp, width, col):
    return pl.BlockSpec((lp, width), lambda b, i: (b, col), pipeline_mode=pl.Buffered(1))


def _fox(qkv, qaug, kaug, batch):
    t = qkv.shape[0]
    lp = t // batch
    tq = SEQ_TILE
    nq = lp // tq
    w = FOX_WIDTH
    qmap = lambda b, i: (b * nq + i, 0)
    return pl.pallas_call(
        _fox_kernel,
        grid=(batch, nq),
        in_specs=[pl.BlockSpec((tq, w), qmap), pl.BlockSpec((tq, w), qmap),
                  _seq_resident(lp, w, 1), _seq_resident(lp, w, 0), _seq_resident(lp, w, 2)],
        out_specs=pl.BlockSpec((tq, w), qmap),
        out_shape=jax.ShapeDtypeStruct((t, w), BF16),
        scratch_shapes=[pltpu.VMEM((FOX_HEADS, tq, LANE), F32), pltpu.VMEM((FOX_HEADS, tq, 2 * HEAD_DIM), F32),
                        pltpu.SMEM((FOX_HEADS,), F32)],
        compiler_params=_params("arbitrary", "arbitrary"),
        name="fox",
    )(qkv, qaug, qkv, kaug, qkv)


def _sb_kernel(q_ref, k_ref, v_ref, o_ref, carry_ref, acc_ref):
    i = pl.program_id(1)
    tq = q_ref.shape[0]
    tk = KV_CHUNK
    nd = tq // tk
    hd = HEAD_DIM
    rr = lax.broadcasted_iota(jnp.int32, (tk, tk), 0)
    cc = lax.broadcasted_iota(jnp.int32, (tk, tk), 1)
    later_mat = jnp.where(rr > cc, 1.0, 0.0).astype(BF16)
    carry_ref[...] = jnp.zeros_like(carry_ref)
    acc_ref[...] = jnp.zeros_like(acc_ref)

    def step(j, r0):
        off = pl.multiple_of(j * tk, tk)
        rows = slice(r0 or 0, tq)
        for h in range(SB_HEADS):
            cols = slice(h * hd, (h + 1) * hd)
            z = lax.dot_general(q_ref[rows, cols], k_ref[pl.ds(off, tk), cols], (((1,), (1,)), ((), ())),
                                preferred_element_type=F32)
            sp = _softplus_scores(z)
            spm = sp
            if r0 is not None:
                qpos = i * tq + r0 + lax.broadcasted_iota(jnp.int32, (tq - r0, 1), 0)
                mask = (j * tk + lax.broadcasted_iota(jnp.int32, (1, tk), 1)) < qpos
                spm = jnp.where(mask, sp, 0.0)
            cum = (jnp.dot(spm.astype(BF16), later_mat, preferred_element_type=F32)
                   + _wide(carry_ref[h, rows, :], tk))
            a = jnp.exp((z - sp) - cum)
            if r0 is not None:
                a = jnp.where(mask, a, 0.0)
            acc_ref[h, rows, :] += jnp.dot(a.astype(BF16), v_ref[pl.ds(off, tk), cols], preferred_element_type=F32)
            carry_ref[h, rows, :] = jnp.broadcast_to(cum[:, 0:1] + spm[:, 0:1], (tq - rows.start, LANE))

    for dd in range(nd - 1, -1, -1):
        step(i * nd + dd, dd * tk)

    def least_carry():
        least = carry_ref[0]
        for h in range(1, SB_HEADS):
            least = jnp.minimum(least, carry_ref[h])
        return jnp.min(least)

    def more(state):
        j, least = state
        return jnp.logical_and(j >= 0, least < EXP_ZERO)

    def body(state):
        j, _ = state
        step(j, None)
        return j - 1, least_carry()

    lax.while_loop(more, body, (i * nd - 1, least_carry()))
    for h in range(SB_HEADS):
        o_ref[:, h * hd:(h + 1) * hd] = acc_ref[h].astype(o_ref.dtype)


def _sb(qkv, batch):
    t = qkv.shape[0]
    lp = t // batch
    tq = SEQ_TILE
    nq = lp // tq
    w = SB_WIDTH
    base = 3 * FOX_WIDTH // w
    return pl.pallas_call(
        _sb_kernel,
        grid=(batch, nq),
        in_specs=[pl.BlockSpec((tq, w), lambda b, i: (b * nq + i, base)),
                  _seq_resident(lp, w, base + 1), _seq_resident(lp, w, base + 2)],
        out_specs=pl.BlockSpec((tq, w), lambda b, i: (b * nq + i, 0)),
        out_shape=jax.ShapeDtypeStruct((t, w), BF16),
        scratch_shapes=[pltpu.VMEM((SB_HEADS, tq, LANE), F32), pltpu.VMEM((SB_HEADS, tq, HEAD_DIM), F32)],
        compiler_params=_params("parallel", "arbitrary"),
        name="stickbreak",
    )(qkv, qkv, qkv)


def _merge_kernel(h_ref, g_ref, wg0_ref, wg1_ref, wg2_ref, wgt_ref, bgate_ref, yf_ref, ys_ref, yl_ref,
                  wfo_ref, wso_ref, wlo_ref, wout_ref, o_ref, wgate_ref):
    @pl.when(pl.program_id(0) == 0)
    def _():
        blocks = (wg0_ref, wg1_ref, wg2_ref, wgt_ref)
        dd = h_ref.shape[1]
        for br in range(N_BRANCH):
            both = jnp.concatenate([blocks[br][...], blocks[br + 1][:, :LANE]], axis=1)
            wgate_ref[:, br * dd:(br + 1) * dd] = both[:, GATE_SHIFT:GATE_SHIFT + dd].astype(BF16)

    h = h_ref[...]
    d = h.shape[1]
    xn = _rms(h, g_ref[...]).astype(BF16)
    merged = None
    for br, (y_ref, w_ref) in enumerate(((yf_ref, wfo_ref), (ys_ref, wso_ref), (yl_ref, wlo_ref))):
        cols = slice(br * d, (br + 1) * d)
        gate = jax.nn.sigmoid(jnp.dot(xn, wgate_ref[:, cols], preferred_element_type=F32) + bgate_ref[:, cols])
        term = gate * jnp.dot(y_ref[...], w_ref[...], preferred_element_type=F32)
        merged = term if merged is None else merged + term
    o_ref[...] = h + jnp.dot(merged.astype(BF16), wout_ref[...], preferred_element_type=F32)


def _merge(h, g, w_in, layer, bgate, yf, ys, yl, wfo, wso, wlo, wout):
    t, d = h.shape
    tm = ROW_TILE
    row = lambda i: (i, 0)
    first = COL_GATE - GATE_SHIFT
    assert first % d == 0
    wspec = lambda width, blk: pl.BlockSpec((None, d, width), lambda i: (layer, 0, blk),
                                            pipeline_mode=pl.Buffered(1))
    return pl.pallas_call(
        _merge_kernel,
        grid=(t // tm,),
        in_specs=[pl.BlockSpec((tm, d), row), _resident((1, d)),
                  wspec(d, first // d), wspec(d, first // d + 1), wspec(d, first // d + 2),
                  wspec(LANE, (first + N_BRANCH * d) // LANE), _resident(bgate.shape),
                  pl.BlockSpec((tm, yf.shape[1]), row), pl.BlockSpec((tm, ys.shape[1]), row),
                  pl.BlockSpec((tm, yl.shape[1]), row),
                  _resident(wfo.shape), _resident(wso.shape), _resident(wlo.shape), _resident(wout.shape)],
        out_specs=pl.BlockSpec((tm, d), row),
        out_shape=jax.ShapeDtypeStruct((t, d), F32),
        scratch_shapes=[pltpu.VMEM((d, N_BRANCH * d), BF16)],
        compiler_params=_params("arbitrary"),
        name="merge",
    )(h, g, w_in, w_in, w_in, w_in, bgate, yf, ys, yl, wfo, wso, wlo, wout)


def _ffn_chunk(ff):
    best = LANE
    for c in range(LANE, 1536 + 1, LANE):
        if ff % c == 0:
            best = c
    return best


def _ffn_kernel(h_ref, g_ref, wg_ref, wu_ref, wd_ref, gfin_ref, o_ref, *, final):
    h = h_ref[...]
    hn = _rms(h, g_ref[...]).astype(BF16)
    ff = wg_ref.shape[1]
    fc = _ffn_chunk(ff)
    out = h
    for c in range(ff // fc):
        cols = slice(c * fc, (c + 1) * fc)
        act = jax.nn.silu(jnp.dot(hn, wg_ref[:, cols], preferred_element_type=F32))
        act = act * jnp.dot(hn, wu_ref[:, cols], preferred_element_type=F32)
        out = out + jnp.dot(act.astype(BF16), wd_ref[cols, :], preferred_element_type=F32)
    if final:
        out = _rms(out, gfin_ref[...])
    o_ref[...] = out


def _ffn(h, g, wg, wu, wd, gfin, final):
    t, d = h.shape
    tm = ROW_TILE
    row = lambda i: (i, 0)
    return pl.pallas_call(
        functools.partial(_ffn_kernel, final=final),
        grid=(t // tm,),
        in_specs=[pl.BlockSpec((tm, d), row), _resident((1, d)), _resident(wg.shape), _resident(wu.shape),
                  _resident(wd.shape), _resident((1, d))],
        out_specs=pl.BlockSpec((tm, d), row),
        out_shape=jax.ShapeDtypeStruct((t, d), F32),
        compiler_params=_params("parallel"),
        name="ffn",
    )(h, g, wg, wu, wd, gfin)


def _router_kernel(h_ref, g_ref, wr_ref, hn_ref, idx_ref, wts_ref, cnt_ref, run_ref):
    tm = h_ref.shape[0]

    @pl.when(pl.program_id(0) == 0)
    def _():
        run_ref[...] = jnp.zeros_like(run_ref)

    hn = _rms(h_ref[...], g_ref[...])
    hn_ref[...] = hn
    logits = jnp.dot(hn, wr_ref[...], preferred_element_type=F32, precision=lax.Precision.HIGHEST)
    lane = lax.broadcasted_iota(jnp.int32, (tm, LANE), 1)
    lg = jnp.where(lane < N_EXPERTS, logits, -jnp.inf)
    v1 = jnp.max(lg, axis=-1, keepdims=True)
    i1 = jnp.min(jnp.where(lg == v1, lane, LANE), axis=-1, keepdims=True)
    lg2 = jnp.where(lane == i1, -jnp.inf, lg)
    v2 = jnp.max(lg2, axis=-1, keepdims=True)
    i2 = jnp.min(jnp.where(lg2 == v2, lane, LANE), axis=-1, keepdims=True)
    e = jnp.exp(v2 - v1)
    w1 = 1.0 / (1.0 + e)
    w2 = e * w1
    sel1 = lane == i1
    sel2 = lane == i2
    onehot = jnp.where(sel1, 1.0, 0.0) + jnp.where(sel2, 1.0, 0.0)
    rr = lax.broadcasted_iota(jnp.int32, (tm, tm), 0)
    cc = lax.broadcasted_iota(jnp.int32, (tm, tm), 1)
    before = jnp.where(cc < rr, 1.0, 0.0).astype(BF16)
    excl = jnp.dot(before, onehot.astype(BF16), preferred_element_type=F32) + run_ref[...]
    r1 = jnp.sum(jnp.where(sel1, excl, 0.0), axis=-1, keepdims=True)
    r2 = jnp.sum(jnp.where(sel2, excl, 0.0), axis=-1, keepdims=True)
    run = run_ref[...] + jnp.sum(onehot, axis=0, keepdims=True)
    run_ref[...] = run
    idx = jnp.where(lane == 0, i1, jnp.where(lane == 1, i2, 0))
    rank = jnp.where(lane == 2, r1, jnp.where(lane == 3, r2, 0.0)).astype(jnp.int32)
    idx_ref[...] = idx + rank
    wts_ref[...] = jnp.where(lane == 0, w1, jnp.where(lane == 1, w2, 0.0))
    cnt_ref[...] = jnp.broadcast_to(run, cnt_ref.shape)


def _router(h, g, wr):
    t, d = h.shape
    tm = ROW_TILE
    row = lambda i: (i, 0)
    return pl.pallas_call(
        _router_kernel,
        grid=(t // tm,),
        in_specs=[pl.BlockSpec((tm, d), row), _resident((1, d)), _resident(wr.shape)],
        out_specs=[pl.BlockSpec((tm, d), row), pl.BlockSpec((tm, LANE), row), pl.BlockSpec((tm, LANE), row),
                   pl.BlockSpec((SUBLANE, LANE), lambda i: (0, 0))],
        out_shape=[jax.ShapeDtypeStruct((t, d), F32), jax.ShapeDtypeStruct((t, LANE), jnp.int32),
                   jax.ShapeDtypeStruct((t, LANE), F32), jax.ShapeDtypeStruct((SUBLANE, LANE), F32)],
        scratch_shapes=[pltpu.VMEM((1, LANE), F32)],
        compiler_params=_params("arbitrary"),
        name="router",
    )(h, g, wr)


def _row_copy(src, s, dst, d, sem):
    return pltpu.make_async_copy(src.at[pl.ds(s, 1)], dst.at[pl.ds(d, 1)], sem)


def _dispatch_kernel(pos_ref, tails_ref, hn_ref, xs_hbm, zero_ref, sem):
    i = pl.program_id(0)
    tm = hn_ref.shape[0]

    @pl.when(i == 0)
    def _():
        zero_ref[...] = jnp.zeros_like(zero_ref)
        tz = zero_ref.shape[0]
        tails = [pl.ds(pl.multiple_of(tails_ref[e], SUBLANE), tz) for e in range(N_EXPERTS)]
        tails += [pl.ds(xs_hbm.shape[0] - (e + 1) * tz, tz) for e in range(N_EXPERTS)]
        for rows in tails:
            clear = pltpu.make_async_copy(zero_ref, xs_hbm.at[rows], sem)
            clear.start()
            clear.wait()

    def issue(g, carry):
        for u in range(ISSUE_UNROLL):
            r = g * ISSUE_UNROLL + u
            _row_copy(hn_ref, r, xs_hbm, pos_ref[i, r], sem).start(priority=0)
            _row_copy(hn_ref, r, xs_hbm, pos_ref[i, tm + r], sem).start(priority=1)
        return carry

    lax.fori_loop(0, tm // ISSUE_UNROLL, issue, 0)
    for _ in range(2):
        pltpu.make_async_copy(hn_ref, xs_hbm.at[pl.ds(0, tm)], sem).wait()


def _dispatch(pos, tails, hn, n_rows):
    nt = pos.shape[0]
    t, d = hn.shape
    return pl.pallas_call(
        _dispatch_kernel,
        grid_spec=pltpu.PrefetchScalarGridSpec(
            num_scalar_prefetch=2, grid=(nt,),
            in_specs=[pl.BlockSpec((t // nt, d), lambda i, p, tl: (i, 0))],
            out_specs=pl.BlockSpec(memory_space=pl.ANY),
            scratch_shapes=[pltpu.VMEM((MOE_TILE, d), F32), pltpu.SemaphoreType.DMA(())]),
        out_shape=jax.ShapeDtypeStruct((n_rows, d), F32),
        compiler_params=pltpu.CompilerParams(dimension_semantics=("arbitrary",), has_side_effects=True,
                                             vmem_limit_bytes=VMEM_LIMIT, disable_bounds_checks=True),
        name="dispatch",
    )(pos, tails, hn)


def _experts_kernel(te_ref, na_ref, xs_ref, wg_ref, wu_ref, wd_ref, ys_ref, xb_ref, acc_ref):
    del te_ref
    i = pl.program_id(0)
    f = pl.program_id(1)

    @pl.when(f == 0)
    def _():
        xb_ref[...] = xs_ref[...].astype(BF16)
        acc_ref[...] = jnp.zeros_like(acc_ref)

    @pl.when(i < na_ref[0])
    def _():
        xb = xb_ref[...]
        act = jax.nn.silu(jnp.dot(xb, wg_ref[...], preferred_element_type=F32))
        act = act * jnp.dot(xb, wu_ref[...], preferred_element_type=F32)
        acc_ref[...] += jnp.dot(act.astype(BF16), wd_ref[...], preferred_element_type=F32)

    @pl.when(f == pl.num_programs(1) - 1)
    def _():
        ys_ref[...] = acc_ref[...]


def _experts(tile_expert, n_active, xs, wg, wu, wd):
    nr, d = xs.shape
    tm = MOE_TILE
    ff = wg.shape[2]
    tf = MOE_FF_TILE if ff % MOE_FF_TILE == 0 else ff
    return pl.pallas_call(
        _experts_kernel,
        grid_spec=pltpu.PrefetchScalarGridSpec(
            num_scalar_prefetch=2, grid=(nr // tm, ff // tf),
            in_specs=[pl.BlockSpec((tm, d), lambda i, f, te, na: (jnp.minimum(i, na[0] - 1), 0)),
                      pl.BlockSpec((None, d, tf), lambda i, f, te, na: (te[i], 0, f)),
                      pl.BlockSpec((None, d, tf), lambda i, f, te, na: (te[i], 0, f)),
                      pl.BlockSpec((None, tf, d), lambda i, f, te, na: (te[i], f, 0))],
            out_specs=pl.BlockSpec((tm, d), lambda i, f, te, na: (i, 0)),
            scratch_shapes=[pltpu.VMEM((tm, d), BF16), pltpu.VMEM((tm, d), F32)]),
        out_shape=jax.ShapeDtypeStruct((nr, d), F32),
        compiler_params=_params("arbitrary", "arbitrary"),
        name="experts",
    )(tile_expert, n_active, xs, wg, wu, wd)


def _combine_kernel(pos_ref, h_ref, wts_ref, gfin_ref, ys_hbm, o_ref, buf_ref, sem, *, final):
    i = pl.program_id(0)
    tm = h_ref.shape[0]

    def gather(step, slot):
        def issue(g, carry):
            for u in range(ISSUE_UNROLL):
                r = g * ISSUE_UNROLL + u
                _row_copy(ys_hbm, pos_ref[step, r], buf_ref.at[slot], r, sem.at[slot]).start(priority=0)
                _row_copy(ys_hbm, pos_ref[step, tm + r], buf_ref.at[slot], tm + r, sem.at[slot]).start(priority=1)
            return carry
        lax.fori_loop(0, tm // ISSUE_UNROLL, issue, 0)

    slot = i % 2

    @pl.when(i == 0)
    def _():
        gather(0, 0)

    @pl.when(i + 1 < pl.num_programs(0))
    def _():
        gather(i + 1, 1 - slot)

    pltpu.make_async_copy(ys_hbm.at[pl.ds(0, 2 * tm)], buf_ref.at[slot], sem.at[slot]).wait()
    wts = wts_ref[...]
    out = h_ref[...] + wts[:, 0:1] * buf_ref[slot, 0:tm, :] + wts[:, 1:2] * buf_ref[slot, tm:2 * tm, :]
    if final:
        out = _rms(out, gfin_ref[...])
    o_ref[...] = out


def _combine(pos, h, wts, gfin, ys, final):
    t, d = h.shape
    nt = pos.shape[0]
    tm = t // nt
    row = lambda i, p: (i, 0)
    return pl.pallas_call(
        functools.partial(_combine_kernel, final=final),
        grid_spec=pltpu.PrefetchScalarGridSpec(
            num_scalar_prefetch=1, grid=(nt,),
            in_specs=[pl.BlockSpec((tm, d), row), pl.BlockSpec((tm, LANE), row),
                      pl.BlockSpec((1, d), lambda i, p: (0, 0)), pl.BlockSpec(memory_space=pl.ANY)],
            out_specs=pl.BlockSpec((tm, d), row),
            scratch_shapes=[pltpu.VMEM((2, 2 * tm, d), F32), pltpu.SemaphoreType.DMA((2,))]),
        out_shape=jax.ShapeDtypeStruct((t, d), F32),
        compiler_params=pltpu.CompilerParams(dimension_semantics=("arbitrary",), vmem_limit_bytes=VMEM_LIMIT,
                                             disable_bounds_checks=True),
        name="combine",
    )(pos, h, wts, gfin, ys)


def _moe(h, g, router_w, wg, wu, wd, gfin, final):
    t, d = h.shape
    tm = MOE_TILE
    wr = jnp.pad(router_w, ((0, 0), (0, LANE - N_EXPERTS)))
    hn, idx, wts, cnt = _router(h, g, wr)
    counts = cnt[0, :N_EXPERTS].astype(jnp.int32)
    padded = ((counts + tm - 1) // tm) * tm
    ends = jnp.cumsum(padded)
    offs = ends - padded
    n_tiles = (2 * t + N_EXPERTS * (tm - 1)) // tm
    pos1 = jnp.take(offs, idx[:, 0]) + idx[:, 2]
    pos2 = jnp.take(offs, idx[:, 1]) + idx[:, 3]
    pos = jnp.concatenate([pos1.reshape(-1, ROW_TILE), pos2.reshape(-1, ROW_TILE)], axis=1)
    tile_start = jnp.arange(n_tiles, dtype=jnp.int32) * tm
    tile_expert = jnp.minimum(jnp.sum(tile_start[:, None] >= ends[None, :], axis=1), N_EXPERTS - 1).astype(jnp.int32)
    n_active = (ends[-1:] // tm).astype(jnp.int32)
    tails = jnp.where(padded > 0, ends - tm, 0).astype(jnp.int32)
    xs = _dispatch(pos, tails, hn, n_tiles * tm)
    ys = _experts(tile_expert, n_active, xs, wg, wu, wd)
    return _combine(pos, h, wts, gfin, ys, final)


def _block_diag(w):
    n, a, b = w.shape
    eye = jnp.eye(n, dtype=w.dtype)
    return jnp.einsum('nij,nm->nimj', w, eye).reshape(n * a, n * b)


def kernel(x, meta_tokens, g_mix, w_in, b_forget, b_gate, conv_w, conv_b, lru_wa, lru_ba, lru_wx, lru_bx, lru_lambda, w_fox_o, w_sb_o, w_lru_o, w_out, g_ffn, ffn_w_gate, ffn_w_up, ffn_w_down, router_w, moe_w_gate, moe_w_up, moe_w_down, g_final):
    batch, seq, d = x.shape
    depth = g_mix.shape[0]
    l = seq + N_META
    lp = -(-l // SEQ_TILE) * SEQ_TILE
    assert (batch * lp) % ROW_TILE == 0 and lp % TIME_TILE == 0 and SEQ_TILE % KV_CHUNK == 0
    meta = jnp.broadcast_to(meta_tokens[None].astype(x.dtype), (batch, N_META, d))
    h = jnp.concatenate([meta, x, jnp.zeros((batch, lp - l, d), x.dtype)], axis=1).reshape(batch * lp, d)

    gfin = g_final.reshape(1, d)
    for layer in range(depth):
        bf = jnp.pad(b_forget[layer], (0, LANE - FOX_HEADS)).reshape(1, LANE)

        qkv, lru, f = _inproj(h, g_mix[layer].reshape(1, d), w_in, layer)
        yl, qaug, kaug = _lru(lru, f, conv_w[layer], conv_b[layer].reshape(1, -1),
                              _block_diag(lru_wa[layer]).astype(BF16), lru_ba[layer].reshape(1, -1),
                              _block_diag(lru_wx[layer]).astype(BF16), lru_bx[layer].reshape(1, -1),
                              lru_lambda[layer].reshape(1, -1), bf, batch)
        yf = _fox(qkv, qaug, kaug, batch)
        ys = _sb(qkv, batch)
        h = _merge(h, g_mix[layer].reshape(1, d), w_in, layer, b_gate[layer].reshape(1, -1), yf, ys, yl,
                   w_fox_o[layer].astype(BF16), w_sb_o[layer].astype(BF16), w_lru_o[layer].astype(BF16),
                   w_out[layer].astype(BF16))

        final = layer == depth - 1
        j = layer // 2
        gf = g_ffn[layer].reshape(1, d)
        if layer % 2 == 0:
            h = _ffn(h, gf, ffn_w_gate[j].astype(BF16), ffn_w_up[j].astype(BF16), ffn_w_down[j].astype(BF16),
                     gfin, final)
        else:
            h = _moe(h, gf, router_w[j], moe_w_gate[j].astype(BF16), moe_w_up[j].astype(BF16),
                     moe_w_down[j].astype(BF16), gfin, final)
    return h.reshape(batch, lp, d)[:, N_META:l]
```
